```python
import jax, jax.numpy as jnp
from jax import lax
import numpy as np

D_MODEL = 1024
BATCH = 2
SEQ = 8192
DEPTH = 2
DEC_BATCH = 128
DEC_SEQ = 4
PAST_LEN = 2048
PAGE_SIZE = 128

HEAD_DIM = 64
N_HEADS = D_MODEL // HEAD_DIM
ATTN_WIDTH = N_HEADS * HEAD_DIM
DIL_GROUPS = ((128, 1), (512, 4), (2048, 16))
N_GROUPS = len(DIL_GROUPS)
BAND_BLOCK = 128
MOBA_BLOCK = 256
MOBA_TOP_K = 3
D_FF = 2816
CONV_WIDTH = 3
ROPE_THETA = 10000.0
NORM_EPS = 1e-6
ATTN_SCALE = HEAD_DIM ** -0.5
N_A_LAYERS = DEPTH // 2
N_B_LAYERS = DEPTH - N_A_LAYERS
QUERY_CHUNK = 64
GATHER_ROWS = 128

kernel_name = 'yoco_dilated_moba_convffn_step'


def rmsnorm(x, gain):
    xf = x.astype(jnp.float32)
    y = xf * lax.rsqrt(jnp.mean(xf * xf, axis=-1, keepdims=True) + NORM_EPS)
    return (y * gain.astype(jnp.float32)).astype(x.dtype)


def rope(x, pos):
    half = HEAD_DIM // 2
    inv_freq = ROPE_THETA ** (-jnp.arange(half, dtype=jnp.float32) / half)
    ang = pos.astype(jnp.float32)[:, None] * inv_freq[None, :]
    cos, sin = jnp.cos(ang)[:, None, :], jnp.sin(ang)[:, None, :]
    xf = x.astype(jnp.float32)
    x1, x2 = xf[..., :half], xf[..., half:]
    return jnp.concatenate([x1 * cos - x2 * sin, x2 * cos + x1 * sin], axis=-1).astype(x.dtype)


def softmax_parts(s):
    m = jnp.max(s, axis=-1, keepdims=True)
    e = jnp.exp(s - m)
    den = jnp.sum(e, axis=-1, keepdims=True)
    return e / den, (m + jnp.log(den))[..., 0]


def dilated_window_prompt(q, k, v, window, dilation):
    b, s_len, h, hd = q.shape
    length = s_len // dilation
    n_back = window // dilation
    n_seq = b * dilation

    def to_strided(t):
        return t.reshape(b, length, dilation, h, hd).transpose(0, 2, 1, 3, 4).reshape(n_seq, length, h, hd)

    n_blk = -(-length // BAND_BLOCK)
    padw = ((0, 0), (0, n_blk * BAND_BLOCK - length), (0, 0), (0, 0))
    qs, ks, vs = [jnp.pad(to_strided(t), padw).reshape(n_seq, n_blk, BAND_BLOCK, h, hd) for t in (q, k, v)]

    def with_prev(t):
        prev = jnp.pad(t[:, :-1], ((0, 0), (1, 0), (0, 0), (0, 0), (0, 0)))
        return jnp.concatenate([prev, t], axis=2)

    kc, vc = with_prev(ks), with_prev(vs)
    s = jnp.einsum('nbqhd,nbkhd->nbhqk', qs, kc, preferred_element_type=jnp.float32) * ATTN_SCALE
    qi = jnp.arange(BAND_BLOCK)[:, None] + BAND_BLOCK
    ki = jnp.arange(2 * BAND_BLOCK)[None, :]
    dist = qi - ki
    in_band = (dist >= 0) & (dist <= n_back)
    key_exists = (jnp.arange(n_blk)[:, None, None] > 0) | (ki[None] >= BAND_BLOCK)
    mask = in_band[None] & key_exists
    s = jnp.where(mask[None, :, None], s, -jnp.inf)
    p, lse = softmax_parts(s)
    o = jnp.einsum('nbhqk,nbkhd->nbqhd', p.astype(v.dtype), vc)
    o = o.reshape(n_seq, n_blk * BAND_BLOCK, h, hd)[:, :length]
    o = o.reshape(b, dilation, length, h, hd).transpose(0, 2, 1, 3, 4).reshape(b, s_len, h, hd)
    lse = lse.transpose(0, 1, 3, 2).reshape(n_seq, n_blk * BAND_BLOCK, h)[:, :length]
    lse = lse.reshape(b, dilation, length, h).transpose(0, 2, 1, 3).reshape(b, s_len, h)
    return o, lse


def dilated_window_sample(q, k_all, v_all, n_buf, window, dilation):
    n_q = q.shape[1]
    n_back = window // dilation
    rows = n_buf + jnp.arange(n_q)[:, None] - dilation * jnp.arange(n_back + 1)[None, :]
    valid = rows >= 0
    rows = jnp.maximum(rows, 0)
    kg, vg = k_all[:, rows], v_all[:, rows]
    s = jnp.einsum('nqhd,nqjhd->nhqj', q, kg, preferred_element_type=jnp.float32) * ATTN_SCALE
    s = jnp.where(valid[None, None], s, -jnp.inf)
    p, lse = softmax_parts(s)
    o = jnp.einsum('nhqj,nqjhd->nqhd', p.astype(v_all.dtype), vg)
    return o, lse.transpose(0, 2, 1)


def to_blocks(k, v):
    n, t_len, h, hd = k.shape
    n_blk = -(-t_len // MOBA_BLOCK)
    padw = ((0, 0), (0, n_blk * MOBA_BLOCK - t_len), (0, 0), (0, 0))
    kb = jnp.pad(k, padw).reshape(n, n_blk, MOBA_BLOCK, h, hd)
    vb = jnp.pad(v, padw).reshape(n, n_blk, MOBA_BLOCK, h, hd)
    kmean = jnp.mean(kb.astype(jnp.float32), axis=2)
    return kb, vb, kmean


def query_chunk(n_rows, n_queries):
    c = QUERY_CHUNK
    while c > 1 and (n_rows * c > GATHER_ROWS or n_queries % c):
        c //= 2
    return c


def moba_attention(q, pos, kb, vb, kmean, q_chunk):
    n, n_q, h, hd = q.shape
    n_blocks = kb.shape[1]
    k_sel = min(MOBA_TOP_K, n_blocks)
    n_chunks = n_q // q_chunk
    q_c = q.reshape(n, n_chunks, q_chunk, h, hd).transpose(1, 0, 2, 3, 4)
    pos_c = pos.reshape(n_chunks, q_chunk)
    n4 = jnp.arange(n)[:, None, None, None]
    h4 = jnp.arange(h)[None, :, None, None]
    n3 = jnp.arange(n)[:, None, None]
    h3 = jnp.arange(h)[None, :, None]

    def attend(args):
        qq, pp = args
        jq = pp // MOBA_BLOCK
        gate = jnp.einsum('nchd,nbhd->nhcb', qq.astype(jnp.float32), kmean)
        fully_past = jnp.arange(n_blocks)[None, :] < jq[:, None]
        gate = jnp.where(fully_past[None, None], gate, -jnp.inf)
        _, sel = lax.top_k(gate, k_sel)
        sel_valid = jnp.arange(k_sel)[None, :] < jq[:, None]
        k_s = kb[n4, sel, :, h4]
        v_s = vb[n4, sel, :, h4]
        s_sel = jnp.einsum('nchd,nhckrd->nhckr', qq, k_s, preferred_element_type=jnp.float32) * ATTN_SCALE
        s_sel = jnp.where(sel_valid[None, None, :, :, None], s_sel, -jnp.inf)
        own = jq[None, None, :]
        k_o = kb[n3, own, :, h3]
        v_o = vb[n3, own, :, h3]
        s_own = jnp.einsum('nchd,nhcrd->nhcr', qq, k_o, preferred_element_type=jnp.float32) * ATTN_SCALE
        key_pos = jq[:, None] * MOBA_BLOCK + jnp.arange(MOBA_BLOCK)[None, :]
        s_own = jnp.where((key_pos <= pp[:, None])[None, None], s_own, -jnp.inf)
        s = jnp.concatenate([s_sel.reshape(n, h, q_chunk, k_sel * MOBA_BLOCK), s_own], axis=-1)
        p = jax.nn.softmax(s, axis=-1).astype(vb.dtype)
        p_sel = p[..., :k_sel * MOBA_BLOCK].reshape(n, h, q_chunk, k_sel, MOBA_BLOCK)
        p_own = p[..., k_sel * MOBA_BLOCK:]
        return jnp.einsum('nhckr,nhckrd->nchd', p_sel, v_s) + jnp.einsum('nhcr,nhcrd->nchd', p_own, v_o)

    out = lax.map(attend, (q_c, pos_c))
    return out.transpose(1, 0, 2, 3, 4).reshape(n, n_q, h, hd)


def shared_kv(h, pos, norm_kv, w_kv_b):
    n, s_len, _ = h.shape
    kv = (rmsnorm(h, norm_kv) @ w_kv_b).reshape(n, s_len, 2, N_HEADS, HEAD_DIM)
    return rope(kv[:, :, 0], pos), kv[:, :, 1]


def conv_ffn(x, buf, w_gate, w_up, conv_w, conv_b, w_down):
    s_len = x.shape[1]
    g = x @ w_gate
    u = x @ w_up
    gp = jnp.concatenate([buf, g], axis=1)
    c = conv_b + sum(conv_w[j] * gp[:, j:j + s_len] for j in range(CONV_WIDTH))
    y = (jax.nn.silu(c) * u) @ w_down
    return y, gp[:, gp.shape[1] - (CONV_WIDTH - 1):]


def forward(x, pos, a_cache, kv_past, conv_bufs, norm_mix, norm_ffn, norm_kv, norm_final,
            w_qkv_a, w_o_a, w_kv_b, w_q_b, w_o_b, w_ffn_gate, w_ffn_up, ffn_conv_w, ffn_conv_b, w_ffn_down):
    n, s_len, _ = x.shape
    q_chunk = query_chunk(n, s_len)
    h = x
    new_k = [[] for _ in DIL_GROUPS]
    new_v = [[] for _ in DIL_GROUPS]
    new_conv = []
    for layer in range(DEPTH):
        xn = rmsnorm(h, norm_mix[layer])
        if layer < N_A_LAYERS:
            qkv = (xn @ w_qkv_a[layer]).reshape(n, s_len, N_GROUPS, 3, N_HEADS, HEAD_DIM)
            outs, lses = [], []
            for g, (window, dil) in enumerate(DIL_GROUPS):
                q = rope(qkv[:, :, g, 0], pos)
                k = rope(qkv[:, :, g, 1], pos)
                v = qkv[:, :, g, 2]
                if a_cache is None:
                    o, lse = dilated_window_prompt(q, k, v, window, dil)
                    keep = min(window, s_len)
                    k_keep, v_keep = k[:, s_len - keep:], v[:, s_len - keep:]
                else:
                    k_buf, v_buf = a_cache[g][0][layer], a_cache[g][1][layer]
                    n_buf = k_buf.shape[1]
                    k_all = jnp.concatenate([k_buf, k], axis=1)
                    v_all = jnp.concatenate([v_buf, v], axis=1)
                    o, lse = dilated_window_sample(q, k_all, v_all, n_buf, window, dil)
                    k_keep, v_keep = k_all[:, s_len:], v_all[:, s_len:]
                outs.append(o)
                lses.append(lse)
                new_k[g].append(k_keep)
                new_v[g].append(v_keep)
            mix = jax.nn.softmax(jnp.stack(lses, axis=0), axis=0).astype(x.dtype)
            o = jnp.einsum('gnsh,gnshd->nshd', mix, jnp.stack(outs, axis=0))
            h = h + o.reshape(n, s_len, ATTN_WIDTH) @ w_o_a[layer]
        else:
            lb = layer - N_A_LAYERS
            q = rope((xn @ w_q_b[lb]).reshape(n, s_len, N_HEADS, HEAD_DIM), pos)
            o = moba_attention(q, pos, kb, vb, kmean, q_chunk)
            h = h + o.reshape(n, s_len, ATTN_WIDTH) @ w_o_b[lb]
        f, buf = conv_ffn(rmsnorm(h, norm_ffn[layer]), conv_bufs[layer], w_ffn_gate[layer], w_ffn_up[layer],
                          ffn_conv_w[layer], ffn_conv_b[layer], w_ffn_down[layer])
        h = h + f
        new_conv.append(buf)
        if layer == N_A_LAYERS - 1:
            k_new, v_new = shared_kv(h, pos, norm_kv, w_kv_b)
            if kv_past is None:
                k_full, v_full = k_new, v_new
            else:
                k_full = jnp.concatenate([kv_past[0], k_new], axis=1)
                v_full = jnp.concatenate([kv_past[1], v_new], axis=1)
            kb, vb, kmean = to_blocks(k_full, v_full)
    y = rmsnorm(h, norm_final)
    a_k = [jnp.stack(t, axis=0) for t in new_k]
    a_v = [jnp.stack(t, axis=0) for t in new_v]
    return y, a_k, a_v, k_new, v_new, jnp.stack(new_conv, axis=0)


def setup_inputs(seed: int = 0) -> dict:
    key = jax.random.key(seed)
    ks = iter(jax.random.split(key, 32))
    f32 = jnp.float32
    n_pages = PAST_LEN // PAGE_SIZE
    n_phys = (DEC_BATCH * n_pages * 5) // 4

    def normal(shape, scale=1.0):
        return jax.random.normal(next(ks), shape, f32) * scale

    def gain(shape):
        return 1.0 + normal(shape, 0.02)

    inp = {}
    inp['x_prompt'] = normal((BATCH, SEQ, D_MODEL))
    inp['x_sample'] = normal((DEC_BATCH, DEC_SEQ, D_MODEL))
    for g, (window, _) in enumerate(DIL_GROUPS):
        rows = min(window, PAST_LEN)
        inp['cache_a_k' + str(g)] = normal((N_A_LAYERS, DEC_BATCH, rows, N_HEADS, HEAD_DIM))
        inp['cache_a_v' + str(g)] = normal((N_A_LAYERS, DEC_BATCH, rows, N_HEADS, HEAD_DIM))
    inp['cache_b_k'] = normal((n_phys, PAGE_SIZE, N_HEADS, HEAD_DIM))
    inp['cache_b_v'] = normal((n_phys, PAGE_SIZE, N_HEADS, HEAD_DIM))
    inp['state_ffn_conv'] = normal((DEPTH, DEC_BATCH, CONV_WIDTH - 1, D_FF))
    perm = jax.random.permutation(next(ks), n_phys).astype(jnp.int32)
    inp['page_table'] = perm[:DEC_BATCH * n_pages].reshape(DEC_BATCH, n_pages)
    inp['norm_mix'] = gain((DEPTH, D_MODEL))
    inp['norm_ffn'] = gain((DEPTH, D_MODEL))
    inp['norm_kv'] = gain((D_MODEL,))
    inp['norm_final'] = gain((D_MODEL,))
    inp['w_qkv_a'] = normal((N_A_LAYERS, D_MODEL, N_GROUPS * 3 * ATTN_WIDTH), D_MODEL ** -0.5)
    inp['w_o_a'] = normal((N_A_LAYERS, ATTN_WIDTH, D_MODEL), ATTN_WIDTH ** -0.5)
    inp['w_kv_b'] = normal((D_MODEL, 2 * ATTN_WIDTH), D_MODEL ** -0.5)
    inp['w_q_b'] = normal((N_B_LAYERS, D_MODEL, ATTN_WIDTH), D_MODEL ** -0.5)
    inp['w_o_b'] = normal((N_B_LAYERS, ATTN_WIDTH, D_MODEL), ATTN_WIDTH ** -0.5)
    inp['w_ffn_gate'] = normal((DEPTH, D_MODEL, D_FF), D_MODEL ** -0.5)
    inp['w_ffn_up'] = normal((DEPTH, D_MODEL, D_FF), D_MODEL ** -0.5)
    inp['ffn_conv_w'] = normal((DEPTH, CONV_WIDTH, D_FF), CONV_WIDTH ** -0.5)
    inp['ffn_conv_b'] = normal((DEPTH, D_FF), 0.01)
    inp['w_ffn_down'] = normal((DEPTH, D_FF, D_MODEL), D_FF ** -0.5)
    return inp


def reference(x_prompt, x_sample, cache_a_k0, cache_a_v0, cache_a_k1, cache_a_v1, cache_a_k2, cache_a_v2,
              cache_b_k, cache_b_v, state_ffn_conv, page_table, norm_mix, norm_ffn, norm_kv, norm_final,
              w_qkv_a, w_o_a, w_kv_b, w_q_b, w_o_b, w_ffn_gate, w_ffn_up, ffn_conv_w, ffn_conv_b, w_ffn_down):
    n_p, seq = x_prompt.shape[0], x_prompt.shape[1]
    n_s, dec_seq = x_sample.shape[0], x_sample.shape[1]
    n_pages = page_table.shape[1]
    past_len = n_pages * PAGE_SIZE
    pos_p = jnp.arange(seq, dtype=jnp.int32)
    pos_s = past_len + jnp.arange(dec_seq, dtype=jnp.int32)

    conv_zero = jnp.zeros((DEPTH, n_p, CONV_WIDTH - 1, D_FF), x_prompt.dtype)
    y_p, pa_k, pa_v, pb_k, pb_v, p_conv = forward(
        x_prompt, pos_p, None, None, conv_zero, norm_mix, norm_ffn, norm_kv, norm_final,
        w_qkv_a, w_o_a, w_kv_b, w_q_b, w_o_b, w_ffn_gate, w_ffn_up, ffn_conv_w, ffn_conv_b, w_ffn_down)

    k_past = cache_b_k[page_table].reshape(n_s, past_len, N_HEADS, HEAD_DIM)
    v_past = cache_b_v[page_table].reshape(n_s, past_len, N_HEADS, HEAD_DIM)
    a_cache = ((cache_a_k0, cache_a_v0), (cache_a_k1, cache_a_v1), (cache_a_k2, cache_a_v2))
    y_s, sa_k, sa_v, sb_k, sb_v, s_conv = forward(
        x_sample, pos_s, a_cache, (k_past, v_past), state_ffn_conv, norm_mix, norm_ffn, norm_kv, norm_final,
        w_qkv_a, w_o_a, w_kv_b, w_q_b, w_o_b, w_ffn_gate, w_ffn_up, ffn_conv_w, ffn_conv_b, w_ffn_down)

    return (y_p, y_s, pa_k[0], pa_v[0], pa_k[1], pa_v[1], pa_k[2], pa_v[2],
            sa_k[0], sa_v[0], sa_k[1], sa_v[1], sa_k[2], sa_v[2],
            pb_k, pb_v, sb_k, sb_v, p_conv, s_conv)
```

```python
import functools
import math

import jax
import jax.numpy as jnp
from jax import lax
from jax.experimental import pallas as pl
from jax.experimental.pallas import tpu as pltpu

F32 = jnp.float32
BF16 = jnp.bfloat16

HEAD_DIM = 64
DIL_GROUPS = ((128, 1), (512, 4), (2048, 16))
BAND_BLOCK = 128
MOBA_BLOCK = 256
MOBA_TOP_K = 3
PAGE_SIZE = 128
CONV_WIDTH = 3
ROPE_THETA = 10000.0
NORM_EPS = 1e-6
ATTN_SCALE = HEAD_DIM ** -0.5

LANES = 128
SUBLANES = 8
VMEM_LIMIT_BYTES = 56 * 1024 * 1024

HEADS_PER_VREG = LANES // HEAD_DIM
SAMPLE_Q_PAD = SUBLANES
NEG_BIG = -1e30


def _cparams(semantics):
    return pltpu.CompilerParams(dimension_semantics=semantics, vmem_limit_bytes=VMEM_LIMIT_BYTES)


def _nt_dot(a, b):
    return lax.dot_general(a, b, (((1,), (1,)), ((), ())), preferred_element_type=F32)


def _split_bf16(x):
    hi = x.astype(BF16)
    lo = (x - hi.astype(F32)).astype(BF16)
    return hi, lo


def _nt_dot_precise(a, b):
    a_hi, a_lo = _split_bf16(a)
    b_hi, b_lo = _split_bf16(b)
    return _nt_dot(a_hi, b_hi) + (_nt_dot(a_hi, b_lo) + _nt_dot(a_lo, b_hi))


def _rmsnorm_val(x, gain):
    r = lax.rsqrt(jnp.mean(x * x, axis=-1, keepdims=True) + NORM_EPS)
    return (x * r) * gain


def _rope_tables(pos):
    half = HEAD_DIM // 2
    inv_freq = ROPE_THETA ** (-jnp.arange(half, dtype=F32) / half)
    ang = pos.astype(F32)[:, None] * inv_freq[None, :]
    cos, sin = jnp.cos(ang), jnp.sin(ang)
    zero = jnp.zeros_like(sin)
    reps = LANES // HEAD_DIM
    cos_t = jnp.tile(jnp.concatenate([cos, cos], axis=1), (1, reps))
    sin_lo = jnp.tile(jnp.concatenate([-sin, zero], axis=1), (1, reps))
    sin_hi = jnp.tile(jnp.concatenate([zero, sin], axis=1), (1, reps))
    return cos_t, sin_lo, sin_hi


def _norm_mm_kernel(x_ref, g_ref, w_ref, cos_ref, slo_ref, shi_ref, o_ref, xn_ref, *, rope_period, rope_count):
    j = pl.program_id(1)

    @pl.when(j == 0)
    def _():
        xn_ref[...] = _rmsnorm_val(x_ref[...], g_ref[...]).astype(BF16)

    y = jnp.dot(xn_ref[...], w_ref[...], preferred_element_type=F32)
    tn = y.shape[1]
    half = HEAD_DIM // 2

    def store_rope():
        cos, slo, shi = cos_ref[...], slo_ref[...], shi_ref[...]
        for c in range(tn // LANES):
            yc = y[:, c * LANES:(c + 1) * LANES]
            o_ref[:, c * LANES:(c + 1) * LANES] = (
                yc * cos + pltpu.roll(yc, LANES - half, 1) * slo + pltpu.roll(yc, half, 1) * shi)

    def store_plain():
        o_ref[...] = y

    if rope_count == rope_period:
        store_rope()
    else:
        is_rope = (j % rope_period) < rope_count
        pl.when(is_rope)(store_rope)
        pl.when(jnp.logical_not(is_rope))(store_plain)


def _norm_matmul(x, gain, w_bf16, tables, *, rope_period, rope_count, tm, tn=1024):
    t, d = x.shape
    n = w_bf16.shape[1]
    kern = functools.partial(_norm_mm_kernel, rope_period=rope_period, rope_count=rope_count)
    tab_spec = pl.BlockSpec((tm, LANES), lambda i, j: (i, 0))
    return pl.pallas_call(
        kern,
        grid=(t // tm, n // tn),
        in_specs=[
            pl.BlockSpec((tm, d), lambda i, j: (i, 0)),
            pl.BlockSpec((1, d), lambda i, j: (0, 0)),
            pl.BlockSpec((d, tn), lambda i, j: (0, j)),
            tab_spec, tab_spec, tab_spec,
        ],
        out_specs=pl.BlockSpec((tm, tn), lambda i, j: (i, j)),
        out_shape=jax.ShapeDtypeStruct((t, n), F32),
        scratch_shapes=[pltpu.VMEM((tm, d), BF16)],
        compiler_params=_cparams(("parallel", "arbitrary")),
        name="norm_matmul",
    )(x, gain.reshape(1, d), w_bf16, *tables)


def _dil_attn_kernel(q_ref, kp_ref, kc_ref, vp_ref, vc_ref, o_ref, l_ref, *, n_back, n_pairs):
    i = pl.program_id(2)
    blk = BAND_BLOCK
    qi = lax.broadcasted_iota(jnp.int32, (blk, 2 * blk), 0) + blk
    ki = lax.broadcasted_iota(jnp.int32, (blk, 2 * blk), 1)
    dist = qi - ki
    mask = (dist >= 0) & (dist <= n_back) & ((ki >= blk) | (i > 0))

    def pair_body(hp, carry):
        off = pl.multiple_of(hp * LANES, LANES)
        q2 = q_ref[0, :, pl.ds(off, LANES)].astype(BF16)
        k2 = jnp.concatenate([kp_ref[0, :, pl.ds(off, LANES)], kc_ref[0, :, pl.ds(off, LANES)]], axis=0).astype(BF16)
        v2 = jnp.concatenate([vp_ref[0, :, pl.ds(off, LANES)], vc_ref[0, :, pl.ds(off, LANES)]], axis=0).astype(BF16)
        outs, lses = [], []
        for hh in range(HEADS_PER_VREG):
            sl = slice(hh * HEAD_DIM, (hh + 1) * HEAD_DIM)
            s = _nt_dot(q2[:, sl], k2[:, sl]) * ATTN_SCALE
            s = jnp.where(mask, s, -jnp.inf)
            m = jnp.max(s, axis=-1, keepdims=True)
            e = jnp.exp(s - m)
            den = jnp.sum(e, axis=-1, keepdims=True)
            o = jnp.dot(e.astype(BF16), v2[:, sl], preferred_element_type=F32) / den
            outs.append(o)
            lses.append(jnp.broadcast_to(m + jnp.log(den), (blk, HEAD_DIM)))
        o_ref[0, :, pl.ds(off, LANES)] = jnp.concatenate(outs, axis=1)
        l_ref[0, :, pl.ds(off, LANES)] = jnp.concatenate(lses, axis=1)
        return carry

    lax.fori_loop(0, n_pairs, pair_body, 0)


def _dilated_attention_prompt(qkv, group, n_seq, seq_len, width, n_groups):
    window, dil = DIL_GROUPS[group]
    length = seq_len // dil
    n_blk = length // BAND_BLOCK
    cols = n_groups * 3
    view = qkv.reshape(n_seq, length, dil * cols * width)
    base = group * 3

    def spec(which, prev):
        if prev:
            return pl.BlockSpec((1, BAND_BLOCK, width), lambda b, r, i: (b, jnp.maximum(i - 1, 0), r * cols + base + which))
        return pl.BlockSpec((1, BAND_BLOCK, width), lambda b, r, i: (b, i, r * cols + base + which))

    out_spec = pl.BlockSpec((1, BAND_BLOCK, width), lambda b, r, i: (b, i, r))
    out_sds = jax.ShapeDtypeStruct((n_seq, length, dil * width), F32)
    kern = functools.partial(_dil_attn_kernel, n_back=window // dil, n_pairs=width // LANES)
    o, lse = pl.pallas_call(
        kern,
        grid=(n_seq, dil, n_blk),
        in_specs=[spec(0, False), spec(1, True), spec(1, False), spec(2, True), spec(2, False)],
        out_specs=[out_spec, out_spec],
        out_shape=[out_sds, out_sds],
        compiler_params=_cparams(("parallel", "parallel", "arbitrary")),
        name="dilated_attn_prompt",
    )(view, view, view, view, view)
    return o.reshape(n_seq * seq_len, width), lse.reshape(n_seq * seq_len, width)


def _merge_oproj_kernel(*refs, n_groups):
    x_ref, w_ref = refs[0], refs[1]
    o_refs = refs[2:2 + n_groups]
    l_refs = refs[2 + n_groups:2 + 2 * n_groups]
    out_ref = refs[-1]
    if n_groups == 1:
        o = o_refs[0][...]
    else:
        ls = [r[...] for r in l_refs]
        m = functools.reduce(jnp.maximum, ls)
        es = [jnp.exp(l - m) for l in ls]
        den = functools.reduce(lambda a, b: a + b, es)
        o = None
        for e, o_ref in zip(es, o_refs):
            term = (e / den) * o_ref[...]
            o = term if o is None else o + term
    out_ref[...] = x_ref[...] + jnp.dot(o.astype(BF16), w_ref[...], preferred_element_type=F32)


def _merge_oproj(x, w_bf16, outs, lses, *, tm):
    t, d = x.shape
    width = w_bf16.shape[0]
    n_groups = len(outs)
    row_spec = pl.BlockSpec((tm, width), lambda i: (i, 0))
    kern = functools.partial(_merge_oproj_kernel, n_groups=n_groups)
    return pl.pallas_call(
        kern,
        grid=(t // tm,),
        in_specs=[pl.BlockSpec((tm, d), lambda i: (i, 0)), pl.BlockSpec((width, d), lambda i: (0, 0))]
        + [row_spec] * (n_groups + len(lses)),
        out_specs=pl.BlockSpec((tm, d), lambda i: (i, 0)),
        out_shape=jax.ShapeDtypeStruct((t, d), F32),
        compiler_params=_cparams(("parallel",)),
        name="merge_oproj",
    )(x, w_bf16, *outs, *lses)


FFN_CHUNK = 256


def _ffn_kernel(*refs, conv_shift, tiles_per_seq, final):
    it = iter(refs)
    h_ref, prev_ref, gn_ref, wg_ref, wu_ref, wd_ref, cw_ref, cb_ref = (next(it) for _ in range(8))
    gf_ref = next(it) if final else None
    out_ref, tail_ref = next(it), next(it)
    y_ref = next(it) if final else None
    xn_ref, acc_ref = next(it), next(it)

    i = pl.program_id(0)
    tm = h_ref.shape[0]
    n_chunks = wg_ref.shape[1] // FFN_CHUNK
    x = h_ref[...]
    xn_ref[...] = _rmsnorm_val(x, gn_ref[...]).astype(BF16)
    acc_ref[...] = jnp.zeros_like(acc_ref)
    if conv_shift == 1:
        xn_prev = _rmsnorm_val(prev_ref[...], gn_ref[...]).astype(BF16)
        has_prev = (i % tiles_per_seq) != 0
        row = lax.broadcasted_iota(jnp.int32, (tm, FFN_CHUNK), 0)

    def chunk_body(c, carry):
        off = pl.multiple_of(c * FFN_CHUNK, FFN_CHUNK)
        wg = wg_ref[:, pl.ds(off, FFN_CHUNK)]
        g = jnp.dot(xn_ref[...], wg, preferred_element_type=F32)
        u = jnp.dot(xn_ref[...], wu_ref[:, pl.ds(off, FFN_CHUNK)], preferred_element_type=F32)
        if conv_shift == 1:
            gp = jnp.dot(xn_prev, wg, preferred_element_type=F32)
            gp = jnp.where(has_prev, gp, 0.0)
            last, last2 = gp[SUBLANES - 1:SUBLANES, :], gp[SUBLANES - 2:SUBLANES - 1, :]
            g1 = jnp.where(row == 0, last, pltpu.roll(g, 1, 0))
            g2 = jnp.where(row == 0, last2, jnp.where(row == 1, last, pltpu.roll(g, 2, 0)))
            tail_ref[0, :, pl.ds(off, FFN_CHUNK)] = g[tm - SUBLANES:, :]
        else:
            ext = jnp.concatenate([prev_ref[:, pl.ds(off, FFN_CHUNK)], g], axis=0)
            g2 = ext[:tm, :]
            g1 = ext[conv_shift:conv_shift + tm, :]
            tail_ref[:, pl.ds(off, FFN_CHUNK)] = ext[tm:, :]
        cw = cw_ref[:, pl.ds(off, FFN_CHUNK)]
        conv = cb_ref[:, pl.ds(off, FFN_CHUNK)] + ((cw[0:1, :] * g2 + cw[1:2, :] * g1) + cw[2:3, :] * g)
        act = (conv * (1.0 / (1.0 + jnp.exp(-conv)))) * u
        acc_ref[...] += jnp.dot(act.astype(BF16), wd_ref[pl.ds(off, FFN_CHUNK), :], preferred_element_type=F32)
        return carry

    lax.fori_loop(0, n_chunks, chunk_body, 0)
    h_out = x + acc_ref[...]
    out_ref[...] = h_out
    if final:
        y_ref[...] = _rmsnorm_val(h_out, gf_ref[...])


def _conv_ffn(h, prev, gain, wg, wu, wd, conv_w, conv_b, final_gain, *, tm, conv_shift, seq_len):
    t, d = h.shape
    d_ff = wg.shape[1]
    final = final_gain is not None
    const2 = lambda i: (0, 0)
    in_specs = [pl.BlockSpec((tm, d), lambda i: (i, 0))]
    args = [h]
    if conv_shift == 1:
        rows8 = tm // SUBLANES
        in_specs.append(pl.BlockSpec((SUBLANES, d), lambda i: (jnp.maximum(i * rows8 - 1, 0), 0)))
        args.append(h)
        tail_shape = jax.ShapeDtypeStruct((t // tm, SUBLANES, d_ff), F32)
        tail_spec = pl.BlockSpec((1, SUBLANES, d_ff), lambda i: (i, 0, 0))
        tiles_per_seq = seq_len // tm
    else:
        assert t == tm
        n_prev = (CONV_WIDTH - 1) * conv_shift
        in_specs.append(pl.BlockSpec((n_prev, d_ff), const2))
        args.append(prev)
        tail_shape = jax.ShapeDtypeStruct((n_prev, d_ff), F32)
        tail_spec = pl.BlockSpec((n_prev, d_ff), const2)
        tiles_per_seq = 1
    resident = pl.Buffered(1)
    in_specs += [
        pl.BlockSpec((1, d), const2),
        pl.BlockSpec((d, d_ff), const2, pipeline_mode=resident),
        pl.BlockSpec((d, d_ff), const2, pipeline_mode=resident),
        pl.BlockSpec((d_ff, d), const2, pipeline_mode=resident),
        pl.BlockSpec((CONV_WIDTH, d_ff), const2), pl.BlockSpec((1, d_ff), const2),
    ]
    args += [gain.reshape(1, d), wg, wu, wd, conv_w, conv_b.reshape(1, d_ff)]
    out_specs = [pl.BlockSpec((tm, d), lambda i: (i, 0)), tail_spec]
    out_shape = [jax.ShapeDtypeStruct((t, d), F32), tail_shape]
    if final:
        in_specs.append(pl.BlockSpec((1, d), const2))
        args.append(final_gain.reshape(1, d))
        out_specs.append(pl.BlockSpec((tm, d), lambda i: (i, 0)))
        out_shape.append(jax.ShapeDtypeStruct((t, d), F32))
    kern = functools.partial(_ffn_kernel, conv_shift=conv_shift, tiles_per_seq=tiles_per_seq, final=final)
    return pl.pallas_call(
        kern,
        grid=(t // tm,),
        in_specs=in_specs,
        out_specs=out_specs,
        out_shape=out_shape,
        scratch_shapes=[pltpu.VMEM((tm, d), BF16), pltpu.VMEM((tm, d), F32)],
        compiler_params=_cparams(("parallel",)),
        name="conv_ffn",
    )(*args)


def _block_mean_kernel(k_ref, o_ref):
    for r in range(o_ref.shape[0]):
        o_ref[r:r + 1, :] = jnp.mean(k_ref[r * MOBA_BLOCK:(r + 1) * MOBA_BLOCK, :], axis=0, keepdims=True)


def _block_mean(kv, width):
    t = kv.shape[0]
    rows = SUBLANES * MOBA_BLOCK
    return pl.pallas_call(
        _block_mean_kernel,
        grid=(t // rows,),
        in_specs=[pl.BlockSpec((rows, width), lambda i: (i, 0))],
        out_specs=pl.BlockSpec((SUBLANES, width), lambda i: (i, 0)),
        out_shape=jax.ShapeDtypeStruct((t // MOBA_BLOCK, width), F32),
        compiler_params=_cparams(("parallel",)),
        name="moba_block_mean",
    )(kv)


def _top_k_mask(gate, n_valid, k_sel):
    nb = gate.shape[1]
    blk_id = lax.broadcasted_iota(jnp.int32, gate.shape, 1)
    sel = jnp.zeros(gate.shape, F32)
    g = gate
    for kk in range(k_sel):
        mx = jnp.max(g, axis=-1, keepdims=True)
        idx = jnp.min(jnp.where(g == mx, blk_id, nb), axis=-1, keepdims=True)
        hit = blk_id == idx
        counts = jnp.where(kk < n_valid, 1.0, 0.0)
        sel = jnp.maximum(sel, jnp.where(hit, counts, 0.0))
        g = jnp.where(hit, -jnp.inf, g)
    return sel


def _moba_prompt_kernel(qi_tab, kj_tab, q_ref, k_ref, v_ref, km_ref, o_ref, sel_ref, m_ref, l_ref, acc_ref, *, n_pairs, k_sel):
    t = pl.program_id(1)
    qi = qi_tab[t]
    kj = kj_tab[t]
    blk = MOBA_BLOCK
    nb = km_ref.shape[1]
    is_own = kj == qi

    @pl.when(kj == 0)
    def _():
        m_ref[...] = jnp.full(m_ref.shape, NEG_BIG, F32)
        l_ref[...] = jnp.zeros_like(l_ref)
        acc_ref[...] = jnp.zeros_like(acc_ref)
        blk_id = lax.broadcasted_iota(jnp.int32, (blk, nb), 1)

        def sel_body(hp, carry):
            off = pl.multiple_of(hp * LANES, LANES)
            q2 = q_ref[:, pl.ds(off, LANES)]
            km2 = km_ref[0, :, pl.ds(off, LANES)]
            for hh in range(HEADS_PER_VREG):
                sl = slice(hh * HEAD_DIM, (hh + 1) * HEAD_DIM)
                gate = _nt_dot_precise(q2[:, sl], km2[:, sl])
                gate = jnp.where(blk_id < qi, gate, -jnp.inf)
                sel_ref[hp * HEADS_PER_VREG + hh] = _top_k_mask(gate, qi, k_sel)
            return carry

        lax.fori_loop(0, n_pairs, sel_body, 0)

    row = lax.broadcasted_iota(jnp.int32, (blk, blk), 0)
    col = lax.broadcasted_iota(jnp.int32, (blk, blk), 1)
    causal = jnp.where(col <= row, 1.0, 0.0)
    blk_id = lax.broadcasted_iota(jnp.int32, (blk, nb), 1)

    def pair_body(hp, carry):
        off = pl.multiple_of(hp * LANES, LANES)
        q2 = q_ref[:, pl.ds(off, LANES)].astype(BF16)
        k2 = k_ref[:, pl.ds(off, LANES)].astype(BF16)
        v2 = v_ref[:, pl.ds(off, LANES)].astype(BF16)
        acc2 = acc_ref[:, pl.ds(off, LANES)]
        new_acc = []
        for hh in range(HEADS_PER_VREG):
            h = hp * HEADS_PER_VREG + hh
            sl = slice(hh * HEAD_DIM, (hh + 1) * HEAD_DIM)
            s = _nt_dot(q2[:, sl], k2[:, sl]) * ATTN_SCALE
            picked = jnp.sum(jnp.where(blk_id == kj, sel_ref[h], 0.0), axis=-1, keepdims=True)
            valid = jnp.where(is_own, causal, picked) > 0.5
            m_old = m_ref[h]
            m_new = jnp.maximum(m_old, jnp.max(jnp.where(valid, s, NEG_BIG), axis=-1, keepdims=True))
            alpha = jnp.exp(m_old - m_new)
            p = jnp.where(valid, jnp.exp(s - m_new), 0.0)
            l_ref[h] = alpha * l_ref[h] + jnp.sum(p, axis=-1, keepdims=True)
            m_ref[h] = m_new
            new_acc.append(alpha * acc2[:, sl] + jnp.dot(p.astype(BF16), v2[:, sl], preferred_element_type=F32))
        acc_ref[:, pl.ds(off, LANES)] = jnp.concatenate(new_acc, axis=1)
        return carry

    lax.fori_loop(0, n_pairs, pair_body, 0)

    @pl.when(is_own)
    def _():
        def fin_body(hp, carry):
            off = pl.multiple_of(hp * LANES, LANES)
            acc2 = acc_ref[:, pl.ds(off, LANES)]
            outs = [acc2[:, hh * HEAD_DIM:(hh + 1) * HEAD_DIM] / l_ref[hp * HEADS_PER_VREG + hh]
                    for hh in range(HEADS_PER_VREG)]
            o_ref[:, pl.ds(off, LANES)] = jnp.concatenate(outs, axis=1)
            return carry

        lax.fori_loop(0, n_pairs, fin_body, 0)


def _moba_prompt(q, kv, kmean, n_seq, seq_len, width):
    nb = seq_len // MOBA_BLOCK
    n_heads = width // HEAD_DIM
    pairs = [(a, b) for a in range(nb) for b in range(a + 1)]
    qi_tab = jnp.asarray([p[0] for p in pairs], jnp.int32)
    kj_tab = jnp.asarray([p[1] for p in pairs], jnp.int32)
    kern = functools.partial(_moba_prompt_kernel, n_pairs=width // LANES, k_sel=min(MOBA_TOP_K, nb))
    grid_spec = pltpu.PrefetchScalarGridSpec(
        num_scalar_prefetch=2,
        grid=(n_seq, len(pairs)),
        in_specs=[
            pl.BlockSpec((MOBA_BLOCK, width), lambda b, t, qt, kt: (b * nb + qt[t], 0)),
            pl.BlockSpec((MOBA_BLOCK, width), lambda b, t, qt, kt: (b * nb + kt[t], 0)),
            pl.BlockSpec((MOBA_BLOCK, width), lambda b, t, qt, kt: (b * nb + kt[t], 1)),
            pl.BlockSpec((1, nb, width), lambda b, t, qt, kt: (b, 0, 0)),
        ],
        out_specs=pl.BlockSpec((MOBA_BLOCK, width), lambda b, t, qt, kt: (b * nb + qt[t], 0)),
        scratch_shapes=[
            pltpu.VMEM((n_heads, MOBA_BLOCK, nb), F32),
            pltpu.VMEM((n_heads, MOBA_BLOCK, 1), F32),
            pltpu.VMEM((n_heads, MOBA_BLOCK, 1), F32),
            pltpu.VMEM((MOBA_BLOCK, width), F32),
        ],
    )
    return pl.pallas_call(
        kern,
        grid_spec=grid_spec,
        out_shape=jax.ShapeDtypeStruct((n_seq * seq_len, width), F32),
        compiler_params=_cparams(("parallel", "arbitrary")),
        name="moba_prompt",
    )(qi_tab, kj_tab, q, kv, kv, kmean)


SAMPLE_HEAD_CHUNK = 4
NEW_ROWS_PER_LANE_TILE = 32


def _sample_window_kernel(q_ref, kn_ref, vn_ref, knt_ref, vnt_ref, kc_ref, vc_ref,
                          ko_ref, vo_ref, o_ref, l_ref, *, window, dil, n_new):
    n = pl.program_id(0)
    n_buf = kc_ref.shape[-1]
    hc = kc_ref.shape[2]
    qp = q_ref.shape[1]
    shift = (LANES - n_new - (n % NEW_ROWS_PER_LANE_TILE) * n_new) % LANES
    lane = lax.broadcasted_iota(jnp.int32, (HEAD_DIM, LANES), 1)
    is_new_lane = lane >= LANES - n_new

    q_idx = lax.broadcasted_iota(jnp.int32, (qp, n_buf), 0)
    r_idx = lax.broadcasted_iota(jnp.int32, (qp, n_buf), 1)
    dist_c = n_buf + q_idx - r_idx
    dil_mask = dil - 1
    valid_c = ((dist_c & dil_mask) == 0) & (dist_c <= window)
    qn_idx = lax.broadcasted_iota(jnp.int32, (qp, qp), 0)
    jn_idx = lax.broadcasted_iota(jnp.int32, (qp, qp), 1)
    dist_n = qn_idx - jn_idx
    valid_n = (dist_n >= 0) & ((dist_n & dil_mask) == 0) & (dist_n <= window) & ((jn_idx < n_new) | (jn_idx == qn_idx))

    outs, lses = [], []
    for h in range(hc):
        sl = slice(h * HEAD_DIM, (h + 1) * HEAD_DIM)
        for c_ref, nt_ref, out_ref in ((kc_ref, knt_ref, ko_ref), (vc_ref, vnt_ref, vo_ref)):
            cache_t = c_ref[0, 0, h]
            rolled = pltpu.roll(cache_t, n_buf - n_new, 1)
            new_t = pltpu.roll(nt_ref[h], shift, 1)
            if n_buf > LANES:
                out_ref[0, 0, h, :, :n_buf - LANES] = rolled[:, :n_buf - LANES]
            out_ref[0, 0, h, :, n_buf - LANES:] = jnp.where(is_new_lane, new_t, rolled[:, n_buf - LANES:])
        qh = q_ref[0, :, sl].astype(BF16)
        s_c = jnp.dot(qh, kc_ref[0, 0, h].astype(BF16), preferred_element_type=F32) * ATTN_SCALE
        s_c = jnp.where(valid_c, s_c, -jnp.inf)
        s_n = _nt_dot(qh, kn_ref[0, :, sl].astype(BF16)) * ATTN_SCALE
        s_n = jnp.where(valid_n, s_n, -jnp.inf)
        m = jnp.maximum(jnp.max(s_c, axis=-1, keepdims=True), jnp.max(s_n, axis=-1, keepdims=True))
        e_c = jnp.exp(s_c - m)
        e_n = jnp.exp(s_n - m)
        den = jnp.sum(e_c, axis=-1, keepdims=True) + jnp.sum(e_n, axis=-1, keepdims=True)
        o = _nt_dot(e_c.astype(BF16), vc_ref[0, 0, h].astype(BF16))
        o = o + jnp.dot(e_n.astype(BF16), vn_ref[0, :, sl].astype(BF16), preferred_element_type=F32)
        outs.append(o / den)
        lses.append(jnp.broadcast_to(m + jnp.log(den), (qp, HEAD_DIM)))
    o_ref[0] = jnp.concatenate(outs, axis=1)
    l_ref[0] = jnp.concatenate(lses, axis=1)


def _sample_window_attention(qkv_seq, qkv_t, cache_k, cache_v, group, n_groups, width, n_new):
    window, dil = DIL_GROUPS[group]
    _, n_seq, n_buf, n_heads, hd = cache_k.shape
    hc = SAMPLE_HEAD_CHUNK
    cw = hc * hd
    per_w = width // cw
    qp = qkv_seq.shape[1]
    ck = jnp.transpose(cache_k, (0, 1, 3, 4, 2))
    cv = jnp.transpose(cache_v, (0, 1, 3, 4, 2))
    qkv_t3 = qkv_t.reshape(n_groups * 3 * n_heads, hd, qkv_t.shape[1])
    base = group * 3

    def seq_spec(which):
        return pl.BlockSpec((1, qp, cw), lambda n, c: (n, 0, (base + which) * per_w + c))

    def t_spec(which):
        return pl.BlockSpec((hc, hd, LANES), lambda n, c: ((base + which) * per_w + c, 0, n // NEW_ROWS_PER_LANE_TILE))

    cache_spec = pl.BlockSpec((1, 1, hc, hd, n_buf), lambda n, c: (0, n, c, 0, 0))
    out_spec = pl.BlockSpec((1, qp, cw), lambda n, c: (n, 0, c))
    kern = functools.partial(_sample_window_kernel, window=window, dil=dil, n_new=n_new)
    ko, vo, o, lse = pl.pallas_call(
        kern,
        grid=(n_seq, n_heads // hc),
        in_specs=[seq_spec(0), seq_spec(1), seq_spec(2), t_spec(1), t_spec(2), cache_spec, cache_spec],
        out_specs=[cache_spec, cache_spec, out_spec, out_spec],
        out_shape=[jax.ShapeDtypeStruct(ck.shape, F32), jax.ShapeDtypeStruct(cv.shape, F32),
                   jax.ShapeDtypeStruct((n_seq, qp, width), F32), jax.ShapeDtypeStruct((n_seq, qp, width), F32)],
        compiler_params=_cparams(("parallel", "parallel")),
        name="sample_window_attn",
    )(qkv_seq, qkv_seq, qkv_seq, qkv_t3, qkv_t3, ck, cv)
    back = (0, 1, 4, 2, 3)
    return o, lse, jnp.transpose(ko, back), jnp.transpose(vo, back)


def _moba_sample_kernel(pt_ref, q_ref, kn_ref, vn_ref, k0_ref, k1_ref, v0_ref, v1_ref, o_ref,
                        m_ref, l_ref, g_ref, acc_ref, *, n_pairs, n_new, k_sel):
    b = pl.program_id(1)
    nb = pl.num_programs(1)
    qp = q_ref.shape[1]

    def pair_body(hp, carry):
        off = pl.multiple_of(hp * LANES, LANES)
        q2 = q_ref[0, :, pl.ds(off, LANES)].astype(BF16)
        accs = []
        for hh in range(HEADS_PER_VREG):
            h = hp * HEADS_PER_VREG + hh
            qh = q2[:, hh * HEAD_DIM:(hh + 1) * HEAD_DIM]
            k_t = jnp.concatenate([k0_ref[0, h], k1_ref[0, h]], axis=1).astype(BF16)
            v_t = jnp.concatenate([v0_ref[0, h], v1_ref[0, h]], axis=1).astype(BF16)
            raw = jnp.dot(qh, k_t, preferred_element_type=F32)
            g_ref[b, h] = jnp.sum(raw, axis=-1, keepdims=True) / MOBA_BLOCK
            s = raw * ATTN_SCALE
            m = jnp.max(s, axis=-1, keepdims=True)
            e = jnp.exp(s - m)
            m_ref[b, h] = m
            l_ref[b, h] = jnp.sum(e, axis=-1, keepdims=True)
            accs.append(_nt_dot(e.astype(BF16), v_t))
        acc_ref[b, :, pl.ds(off, LANES)] = jnp.concatenate(accs, axis=1)
        return carry

    lax.fori_loop(0, n_pairs, pair_body, 0)

    @pl.when(b == nb - 1)
    def _():
        n_blocks = m_ref.shape[0]
        qn_idx = lax.broadcasted_iota(jnp.int32, (qp, qp), 0)
        jn_idx = lax.broadcasted_iota(jnp.int32, (qp, qp), 1)
        valid_own = (jn_idx <= qn_idx) & ((jn_idx < n_new) | (jn_idx == qn_idx))
        blk_lane = lax.broadcasted_iota(jnp.int32, (qp, LANES), 1)

        def fin_body(hp, carry):
            off = pl.multiple_of(hp * LANES, LANES)
            q2 = q_ref[0, :, pl.ds(off, LANES)].astype(BF16)
            kn2 = kn_ref[0, :, pl.ds(off, LANES)].astype(BF16)
            vn2 = vn_ref[0, :, pl.ds(off, LANES)].astype(BF16)
            outs = []
            for hh in range(HEADS_PER_VREG):
                h = hp * HEADS_PER_VREG + hh
                sl = slice(hh * HEAD_DIM, (hh + 1) * HEAD_DIM)
                gate = jnp.full((qp, LANES), -jnp.inf, F32)
                m_blk = jnp.full((qp, LANES), NEG_BIG, F32)
                l_blk = jnp.zeros((qp, LANES), F32)
                for bb in range(n_blocks):
                    gate = jnp.where(blk_lane == bb, g_ref[bb, h], gate)
                    m_blk = jnp.where(blk_lane == bb, m_ref[bb, h], m_blk)
                    l_blk = jnp.where(blk_lane == bb, l_ref[bb, h], l_blk)
                picked = _top_k_mask(gate, n_blocks, k_sel) > 0.5
                s_own = jnp.where(valid_own, _nt_dot(q2[:, sl], kn2[:, sl]) * ATTN_SCALE, -jnp.inf)
                m_all = jnp.maximum(jnp.max(s_own, axis=-1, keepdims=True),
                                    jnp.max(jnp.where(picked, m_blk, NEG_BIG), axis=-1, keepdims=True))
                e_own = jnp.exp(s_own - m_all)
                w_blk = jnp.where(picked, jnp.exp(m_blk - m_all), 0.0)
                den = jnp.sum(e_own, axis=-1, keepdims=True) + jnp.sum(w_blk * l_blk, axis=-1, keepdims=True)
                num = jnp.dot(e_own.astype(BF16), vn2[:, sl], preferred_element_type=F32)
                for bb in range(n_blocks):
                    w = jnp.sum(jnp.where(blk_lane == bb, w_blk, 0.0), axis=-1, keepdims=True)
                    num = num + w * acc_ref[bb, :, pl.ds(off, LANES)][:, sl]
                outs.append(num / den)
            o_ref[0, :, pl.ds(off, LANES)] = jnp.concatenate(outs, axis=1)
            return carry

        lax.fori_loop(0, n_pairs, fin_body, 0)


def _moba_sample(q_seq, k_seq, v_seq, cache_k, cache_v, page_table, width, n_new):
    n_seq, n_pages = page_table.shape
    n_heads = width // HEAD_DIM
    pages_per_block = MOBA_BLOCK // PAGE_SIZE
    assert pages_per_block == 2 and n_pages % pages_per_block == 0 and n_new <= SAMPLE_Q_PAD
    n_blocks = n_pages // pages_per_block
    qp = q_seq.shape[1]
    ck = jnp.transpose(cache_k, (0, 2, 3, 1))
    cv = jnp.transpose(cache_v, (0, 2, 3, 1))
    pt = page_table.reshape(-1).astype(jnp.int32)

    def page_spec(which):
        return pl.BlockSpec((1, n_heads, HEAD_DIM, PAGE_SIZE),
                            lambda n, b, pt_ref: (pt_ref[n * n_pages + b * pages_per_block + which], 0, 0, 0))

    seq_spec = pl.BlockSpec((1, qp, width), lambda n, b, pt_ref: (n, 0, 0))
    kern = functools.partial(_moba_sample_kernel, n_pairs=width // LANES, n_new=n_new,
                             k_sel=min(MOBA_TOP_K, n_blocks + 1))
    grid_spec = pltpu.PrefetchScalarGridSpec(
        num_scalar_prefetch=1,
        grid=(n_seq, n_blocks),
        in_specs=[seq_spec, seq_spec, seq_spec, page_spec(0), page_spec(1), page_spec(0), page_spec(1)],
        out_specs=seq_spec,
        scratch_shapes=[
            pltpu.VMEM((n_blocks, n_heads, qp, 1), F32),
            pltpu.VMEM((n_blocks, n_heads, qp, 1), F32),
            pltpu.VMEM((n_blocks, n_heads, qp, 1), F32),
            pltpu.VMEM((n_blocks, qp, width), F32),
        ],
    )
    return pl.pallas_call(
        kern,
        grid_spec=grid_spec,
        out_shape=jax.ShapeDtypeStruct((n_seq, qp, width), F32),
        compiler_params=_cparams(("parallel", "arbitrary")),
        name="moba_sample",
    )(pt, q_seq, k_seq, v_seq, ck, ck, cv, cv)


def _row_tile(t, cap):
    tm = min(t, cap)
    assert t % tm == 0
    return tm


def _pad_rows(x, rows):
    return jnp.pad(x, ((0, 0), (0, rows - x.shape[1]), (0, 0)))


def kernel(x_prompt, x_sample, cache_a_k0, cache_a_v0, cache_a_k1, cache_a_v1, cache_a_k2, cache_a_v2, cache_b_k, cache_b_v, state_ffn_conv, page_table, norm_mix, norm_ffn, norm_kv, norm_final, w_qkv_a, w_o_a, w_kv_b, w_q_b, w_o_b, w_ffn_gate, w_ffn_up, ffn_conv_w, ffn_conv_b, w_ffn_down):
    n_p, seq, d = x_prompt.shape
    n_s, dec_seq, _ = x_sample.shape
    n_groups = len(DIL_GROUPS)
    width = w_o_a.shape[1]
    n_heads = width // HEAD_DIM
    d_ff = w_ffn_gate.shape[2]
    past_len = page_table.shape[1] * PAGE_SIZE
    assert w_qkv_a.shape[0] == 1 and w_q_b.shape[0] == 1, "one self-decoder and one cross-decoder layer"
    assert seq % (DIL_GROUPS[-1][1] * BAND_BLOCK) == 0 and seq % (SUBLANES * MOBA_BLOCK) == 0
    assert (n_s * dec_seq) % LANES == 0 and n_s % NEW_ROWS_PER_LANE_TILE == 0
    assert NEW_ROWS_PER_LANE_TILE * dec_seq == LANES and d_ff % FFN_CHUNK == 0
    assert all(dil & (dil - 1) == 0 and window // dil == BAND_BLOCK for window, dil in DIL_GROUPS)

    wqkv = w_qkv_a[0].astype(BF16)
    wo_a = w_o_a[0].astype(BF16)
    wkv = w_kv_b.astype(BF16)
    wq_b = w_q_b[0].astype(BF16)
    wo_b = w_o_b[0].astype(BF16)
    wg = w_ffn_gate.astype(BF16)
    wu = w_ffn_up.astype(BF16)
    wd = w_ffn_down.astype(BF16)

    t_p = n_p * seq
    xp = x_prompt.reshape(t_p, d)
    tab_p = _rope_tables(jnp.tile(jnp.arange(seq, dtype=jnp.int32), n_p))
    tm_mm = _row_tile(t_p, 1024)
    tm_p = _row_tile(seq, 512)

    qkv = _norm_matmul(xp, norm_mix[0], wqkv, tab_p, rope_period=3, rope_count=2, tm=tm_mm)
    outs, lses = [], []
    for g in range(n_groups):
        o, lse = _dilated_attention_prompt(qkv, g, n_p, seq, width, n_groups)
        outs.append(o)
        lses.append(lse)
    h = _merge_oproj(xp, wo_a, outs, lses, tm=tm_p)
    qkv5 = qkv.reshape(n_p, seq, n_groups, 3, n_heads, HEAD_DIM)
    pa = []
    for g, (window, _) in enumerate(DIL_GROUPS):
        keep = min(window, seq)
        pa.append(qkv5[None, :, seq - keep:, g, 1])
        pa.append(qkv5[None, :, seq - keep:, g, 2])

    h, tail0 = _conv_ffn(h, None, norm_ffn[0], wg[0], wu[0], wd[0], ffn_conv_w[0], ffn_conv_b[0], None,
                         tm=tm_p, conv_shift=1, seq_len=seq)
    kv = _norm_matmul(h, norm_kv, wkv, tab_p, rope_period=2, rope_count=1, tm=tm_mm)
    kmean = _block_mean(kv, width).reshape(n_p, seq // MOBA_BLOCK, width)
    qb = _norm_matmul(h, norm_mix[1], wq_b, tab_p, rope_period=1, rope_count=1, tm=tm_mm)
    o = _moba_prompt(qb, kv, kmean, n_p, seq, width)
    h = _merge_oproj(h, wo_b, [o], [], tm=tm_p)
    h, tail1, y_p = _conv_ffn(h, None, norm_ffn[1], wg[1], wu[1], wd[1], ffn_conv_w[1], ffn_conv_b[1], norm_final,
                              tm=tm_p, conv_shift=1, seq_len=seq)
    y_p = y_p.reshape(n_p, seq, d)
    kv4 = kv.reshape(n_p, seq, 2, n_heads, HEAD_DIM)
    pb_k, pb_v = kv4[:, :, 0], kv4[:, :, 1]
    tiles_per_seq = seq // tm_p

    def prompt_conv_state(tail):
        last = tail.reshape(n_p, tiles_per_seq, SUBLANES, d_ff)[:, -1]
        return last[:, SUBLANES - (CONV_WIDTH - 1):]

    p_conv = jnp.stack([prompt_conv_state(tail0), prompt_conv_state(tail1)], axis=0)

    t_s = n_s * dec_seq
    xs = jnp.transpose(x_sample, (1, 0, 2)).reshape(t_s, d)
    pos_s = past_len + jnp.repeat(jnp.arange(dec_seq, dtype=jnp.int32), n_s)
    tab_s = _rope_tables(pos_s)

    def to_seq_major(a):
        a = jnp.transpose(a.reshape(dec_seq, n_s, a.shape[-1]), (1, 0, 2))
        return _pad_rows(a, SAMPLE_Q_PAD)

    def to_time_major(a):
        return jnp.transpose(a[:, :dec_seq], (1, 0, 2)).reshape(t_s, a.shape[-1])

    def conv_prev(state):
        return jnp.transpose(state, (1, 0, 2)).reshape((CONV_WIDTH - 1) * n_s, d_ff)

    def conv_next(tail):
        return jnp.transpose(tail.reshape(CONV_WIDTH - 1, n_s, d_ff), (1, 0, 2))

    qkv_s = _norm_matmul(xs, norm_mix[0], wqkv, tab_s, rope_period=3, rope_count=2, tm=t_s)
    qkv_seq = to_seq_major(qkv_s)
    qkv_t = jnp.transpose(qkv_seq[:, :dec_seq], (2, 0, 1)).reshape(qkv_s.shape[1], t_s)
    caches = ((cache_a_k0, cache_a_v0), (cache_a_k1, cache_a_v1), (cache_a_k2, cache_a_v2))
    outs, lses, sa = [], [], []
    for g in range(n_groups):
        o, lse, ko, vo = _sample_window_attention(qkv_seq, qkv_t, caches[g][0], caches[g][1], g, n_groups, width, dec_seq)
        outs.append(to_time_major(o))
        lses.append(to_time_major(lse))
        sa += [ko, vo]
    hs = _merge_oproj(xs, wo_a, outs, lses, tm=t_s)
    hs, s_tail0 = _conv_ffn(hs, conv_prev(state_ffn_conv[0]), norm_ffn[0], wg[0], wu[0], wd[0], ffn_conv_w[0],
                            ffn_conv_b[0], None, tm=t_s, conv_shift=n_s, seq_len=dec_seq)
    kv_s = _norm_matmul(hs, norm_kv, wkv, tab_s, rope_period=2, rope_count=1, tm=t_s)
    qb_s = _norm_matmul(hs, norm_mix[1], wq_b, tab_s, rope_period=1, rope_count=1, tm=t_s)
    kv_seq = to_seq_major(kv_s)
    o = _moba_sample(to_seq_major(qb_s), kv_seq[:, :, :width], kv_seq[:, :, width:], cache_b_k, cache_b_v,
                     page_table, width, dec_seq)
    hs = _merge_oproj(hs, wo_b, [to_time_major(o)], [], tm=t_s)
    hs, s_tail1, y_s = _conv_ffn(hs, conv_prev(state_ffn_conv[1]), norm_ffn[1], wg[1], wu[1], wd[1], ffn_conv_w[1],
                                 ffn_conv_b[1], norm_final, tm=t_s, conv_shift=n_s, seq_len=dec_seq)
    y_s = jnp.transpose(y_s.reshape(dec_seq, n_s, d), (1, 0, 2))
    kv_s4 = kv_seq[:, :dec_seq].reshape(n_s, dec_seq, 2, n_heads, HEAD_DIM)
    sb_k, sb_v = kv_s4[:, :, 0], kv_s4[:, :, 1]
    s_conv = jnp.stack([conv_next(s_tail0), conv_next(s_tail1)], axis=0)

    return (y_p, y_s, *pa, *sa, pb_k, pb_v, sb_k, sb_v, p_conv, s_conv)
```

```python
import functools

import jax
import jax.numpy as jnp
from jax import lax
from jax.experimental import pallas as pl
from jax.experimental.pallas import tpu as pltpu

F32 = jnp.float32
BF16 = jnp.bfloat16

HEAD_DIM = 64
DIL_GROUPS = ((128, 1), (512, 4), (2048, 16))
BAND_BLOCK = 128
MOBA_BLOCK = 256
MOBA_TOP_K = 3
PAGE_SIZE = 128
CONV_WIDTH = 3
ROPE_THETA = 10000.0
NORM_EPS = 1e-6
ATTN_SCALE = HEAD_DIM ** -0.5

LANES = 128
SUBLANES = 8
VMEM_LIMIT_BYTES = 56 * 1024 * 1024

HEADS_PER_VREG = LANES // HEAD_DIM
SAMPLE_Q_PAD = SUBLANES
NEG_BIG = -1e30
MASK_BIAS = -2e30


def _cparams(semantics):
    return pltpu.CompilerParams(dimension_semantics=semantics, vmem_limit_bytes=VMEM_LIMIT_BYTES)


def _nt_dot(a, b):
    return lax.dot_general(a, b, (((1,), (1,)), ((), ())), preferred_element_type=F32)


def _bdot(a, b, ca, cb):
    return lax.dot_general(a, b, (((ca,), (cb,)), ((0,), (0,))), preferred_element_type=F32)


def _split_bf16(x):
    hi = x.astype(BF16)
    lo = (x - hi.astype(F32)).astype(BF16)
    return hi, lo


def _dot_precise(a, b):
    a_hi, a_lo = _split_bf16(a)
    b_hi, b_lo = _split_bf16(b)
    dot = functools.partial(jnp.dot, preferred_element_type=F32)
    return dot(a_hi, b_hi) + (dot(a_hi, b_lo) + dot(a_lo, b_hi))


def _rmsnorm_val(x, gain):
    r = lax.rsqrt(jnp.mean(x * x, axis=-1, keepdims=True) + NORM_EPS)
    return (x * r) * gain


def _rope_tables(pos):
    half = HEAD_DIM // 2
    inv_freq = ROPE_THETA ** (-jnp.arange(half, dtype=F32) / half)
    ang = pos.astype(F32)[:, None] * inv_freq[None, :]
    cos, sin = jnp.cos(ang), jnp.sin(ang)
    zero = jnp.zeros_like(sin)
    reps = LANES // HEAD_DIM
    cos_t = jnp.tile(jnp.concatenate([cos, cos], axis=1), (1, reps))
    sin_lo = jnp.tile(jnp.concatenate([-sin, zero], axis=1), (1, reps))
    sin_hi = jnp.tile(jnp.concatenate([zero, sin], axis=1), (1, reps))
    return (cos_t, sin_lo, sin_hi), (cos.T, sin.T)


def _norm_mm_kernel(x_ref, g_ref, w_ref, cos_ref, slo_ref, shi_ref, o_ref, xn_ref, *, rope_period, rope_count):
    j = pl.program_id(1)

    @pl.when(j == 0)
    def _():
        xn_ref[...] = _rmsnorm_val(x_ref[...], g_ref[...]).astype(BF16)

    y = jnp.dot(xn_ref[...], w_ref[...], preferred_element_type=F32)
    tn = y.shape[1]
    half = HEAD_DIM // 2

    def store_rope():
        cos, slo, shi = cos_ref[...], slo_ref[...], shi_ref[...]
        for c in range(tn // LANES):
            yc = y[:, c * LANES:(c + 1) * LANES]
            o_ref[:, c * LANES:(c + 1) * LANES] = (
                yc * cos + pltpu.roll(yc, LANES - half, 1) * slo + pltpu.roll(yc, half, 1) * shi)

    def store_plain():
        o_ref[...] = y

    if rope_count == rope_period:
        store_rope()
    else:
        is_rope = (j % rope_period) < rope_count
        pl.when(is_rope)(store_rope)
        pl.when(jnp.logical_not(is_rope))(store_plain)


def _norm_matmul(x, gain, w_bf16, tables, *, rope_period, rope_count, tm, n_out=None, tn=1024):
    t, d = x.shape
    n = w_bf16.shape[1] if n_out is None else n_out
    kern = functools.partial(_norm_mm_kernel, rope_period=rope_period, rope_count=rope_count)
    tab_spec = pl.BlockSpec((tm, LANES), lambda i, j: (i, 0))
    return pl.pallas_call(
        kern,
        grid=(t // tm, n // tn),
        in_specs=[
            pl.BlockSpec((tm, d), lambda i, j: (i, 0)),
            pl.BlockSpec((1, d), lambda i, j: (0, 0)),
            pl.BlockSpec((d, tn), lambda i, j: (0, j)),
            tab_spec, tab_spec, tab_spec,
        ],
        out_specs=pl.BlockSpec((tm, tn), lambda i, j: (i, j)),
        out_shape=jax.ShapeDtypeStruct((t, n), F32),
        scratch_shapes=[pltpu.VMEM((tm, d), BF16)],
        compiler_params=_cparams(("parallel", "arbitrary")),
        name="norm_matmul",
    )(x, gain.reshape(1, d), w_bf16, *tables)


def _proj_fm_kernel(x_ref, g_ref, wt_ref, cos_ref, sin_ref, k_ref, v_ref, xn_ref):
    j = pl.program_id(1)
    half = HEAD_DIM // 2

    @pl.when(j == 0)
    def _():
        xn_ref[...] = _rmsnorm_val(x_ref[...], g_ref[...]).astype(BF16)

    def project():
        y = _nt_dot(wt_ref[...], xn_ref[...])
        return y.reshape(y.shape[0] // HEAD_DIM, HEAD_DIM, y.shape[1])

    @pl.when(j == 0)
    def _():
        y = project()
        cos, sin = cos_ref[...], sin_ref[...]
        x1, x2 = y[:, :half, :], y[:, half:, :]
        k_ref[0] = jnp.concatenate([x1 * cos - x2 * sin, x2 * cos + x1 * sin], axis=1)

    @pl.when(j == 1)
    def _():
        v_ref[0] = project()


def _proj_feature_major(x, gain, wt_bf16, w_block, tables_fm, n_seq, seq_len, keep, *, tr):
    d = x.shape[1]
    width = 1024
    n_heads = width // HEAD_DIM
    tiles = keep // tr
    seq_tiles = seq_len // tr
    first = seq_tiles - tiles

    def rows(i):
        return (i // tiles) * seq_tiles + first + (i % tiles)

    out_spec = pl.BlockSpec((1, n_heads, HEAD_DIM, tr), lambda i, j: (i // tiles, 0, 0, i % tiles))
    out_sds = jax.ShapeDtypeStruct((n_seq, n_heads, HEAD_DIM, keep), F32)
    tab_spec = pl.BlockSpec((HEAD_DIM // 2, tr), lambda i, j: (0, rows(i)))
    return pl.pallas_call(
        _proj_fm_kernel,
        grid=(n_seq * tiles, 2),
        in_specs=[
            pl.BlockSpec((tr, d), lambda i, j: (rows(i), 0)),
            pl.BlockSpec((1, d), lambda i, j: (0, 0)),
            pl.BlockSpec((width, d), lambda i, j: (w_block + j, 0)),
            tab_spec, tab_spec,
        ],
        out_specs=[out_spec, out_spec],
        out_shape=[out_sds, out_sds],
        scratch_shapes=[pltpu.VMEM((tr, d), BF16)],
        compiler_params=_cparams(("parallel", "arbitrary")),
        name="proj_feature_major",
    )(x, gain.reshape(1, d), wt_bf16, *tables_fm)


DIL_CHUNK_ROWS = 2048
DIL_HEAD_LANES = LANES


def _dil_attn_kernel(q_ref, k_ref, v_ref, kp_ref, vp_ref, o_ref, l_ref, *, dil, n_back):
    i = pl.program_id(1)
    blk = BAND_BLOCK
    span = blk * dil
    n_units = q_ref.shape[0] // blk
    qi = lax.broadcasted_iota(jnp.int32, (blk, 2 * blk), 0) + blk
    ki = lax.broadcasted_iota(jnp.int32, (blk, 2 * blk), 1)
    dist = qi - ki
    in_band = (dist >= 0) & (dist <= n_back)
    is_cur = ki >= blk

    def rows(ref, start):
        if dil == 1:
            return ref[pl.ds(pl.multiple_of(start, blk), blk), :]
        return ref[pl.ds(start, blk, stride=dil), :]

    def unit_body(u, carry):
        sb = u // dil
        r = u % dil
        start = sb * span + r
        prev_here = sb > 0
        prev_start = jnp.maximum(start - span, 0)
        q4 = rows(q_ref, start).astype(BF16)
        k_prev = jnp.where(prev_here, rows(k_ref, prev_start), rows(kp_ref, r))
        v_prev = jnp.where(prev_here, rows(v_ref, prev_start), rows(vp_ref, r))
        k4 = jnp.concatenate([k_prev, rows(k_ref, start)], axis=0).astype(BF16)
        v4 = jnp.concatenate([v_prev, rows(v_ref, start)], axis=0).astype(BF16)
        mask = in_band & (is_cur | (prev_here | (i > 0)))
        outs, lses = [], []
        for h in range(q4.shape[1] // HEAD_DIM):
            sl = slice(h * HEAD_DIM, (h + 1) * HEAD_DIM)
            s = _nt_dot(q4[:, sl], k4[:, sl]) * ATTN_SCALE
            s = jnp.where(mask, s, -jnp.inf)
            m = jnp.max(s, axis=-1, keepdims=True)
            e = jnp.exp(s - m)
            den = jnp.sum(e, axis=-1, keepdims=True)
            outs.append(jnp.dot(e.astype(BF16), v4[:, sl], preferred_element_type=F32) / den)
            lses.append(jnp.broadcast_to(m + jnp.log(den), (blk, HEAD_DIM)))
        o_val = jnp.concatenate(outs, axis=1)
        l_val = jnp.concatenate(lses, axis=1)
        if dil == 1:
            o_ref[pl.ds(pl.multiple_of(start, blk), blk), :] = o_val
            l_ref[pl.ds(pl.multiple_of(start, blk), blk), :] = l_val
        else:
            o_ref[pl.ds(start, blk, stride=dil), :] = o_val
            l_ref[pl.ds(start, blk, stride=dil), :] = l_val
        return carry

    lax.fori_loop(0, n_units, unit_body, 0, unroll=4)


def _dilated_attention_prompt(qkv, group, n_seq, seq_len, width, n_groups):
    window, dil = DIL_GROUPS[group]
    rows, hl = DIL_CHUNK_ROWS, DIL_HEAD_LANES
    span = BAND_BLOCK * dil
    chunks = seq_len // rows
    per_w = width // hl
    base = group * 3

    def cur(which):
        return pl.BlockSpec((rows, hl), lambda b, i, c: (b * chunks + i, (base + which) * per_w + c))

    def prev(which):
        return pl.BlockSpec(
            (span, hl),
            lambda b, i, c: (jnp.maximum((b * chunks + i) * (rows // span) - 1, 0), (base + which) * per_w + c))

    out_spec = pl.BlockSpec((rows, hl), lambda b, i, c: (b * chunks + i, c))
    out_sds = jax.ShapeDtypeStruct((n_seq * seq_len, width), F32)
    kern = functools.partial(_dil_attn_kernel, dil=dil, n_back=window // dil)
    return pl.pallas_call(
        kern,
        grid=(n_seq, chunks, per_w),
        in_specs=[cur(0), cur(1), cur(2), prev(1), prev(2)],
        out_specs=[out_spec, out_spec],
        out_shape=[out_sds, out_sds],
        compiler_params=_cparams(("parallel", "parallel", "parallel")),
        name="dilated_attn_prompt",
    )(qkv, qkv, qkv, qkv, qkv)


def _merge_oproj_kernel(*refs, n_groups):
    x_ref, w_ref = refs[0], refs[1]
    o_refs = refs[2:2 + n_groups]
    l_refs = refs[2 + n_groups:2 + 2 * n_groups]
    out_ref = refs[-1]
    if n_groups == 1:
        o = o_refs[0][...]
    else:
        ls = [r[...] for r in l_refs]
        m = functools.reduce(jnp.maximum, ls)
        es = [jnp.exp(l - m) for l in ls]
        den = functools.reduce(lambda a, b: a + b, es)
        o = None
        for e, o_ref in zip(es, o_refs):
            term = (e / den) * o_ref[...]
            o = term if o is None else o + term
    out_ref[...] = x_ref[...] + jnp.dot(o.astype(BF16), w_ref[...], preferred_element_type=F32)


def _merge_oproj(x, w_bf16, outs, lses, *, tm):
    t, d = x.shape
    width = w_bf16.shape[0]
    n_groups = len(outs)
    row_spec = pl.BlockSpec((tm, width), lambda i: (i, 0))
    kern = functools.partial(_merge_oproj_kernel, n_groups=n_groups)
    return pl.pallas_call(
        kern,
        grid=(t // tm,),
        in_specs=[pl.BlockSpec((tm, d), lambda i: (i, 0)), pl.BlockSpec((width, d), lambda i: (0, 0))]
        + [row_spec] * (n_groups + len(lses)),
        out_specs=pl.BlockSpec((tm, d), lambda i: (i, 0)),
        out_shape=jax.ShapeDtypeStruct((t, d), F32),
        compiler_params=_cparams(("parallel",)),
        name="merge_oproj",
    )(x, w_bf16, *outs, *lses)


FFN_CHUNK = 256


def _ffn_kernel(*refs, conv_shift, tiles_per_seq, final):
    it = iter(refs)
    h_ref, prev_ref, gn_ref, wg_ref, wu_ref, wd_ref, cw_ref, cb_ref = (next(it) for _ in range(8))
    gf_ref = next(it) if final else None
    out_ref, tail_ref = next(it), next(it)
    y_ref = next(it) if final else None
    xn_ref, acc_ref = next(it), next(it)

    i = pl.program_id(0)
    tm = h_ref.shape[0]
    n_chunks = wg_ref.shape[1] // FFN_CHUNK
    x = h_ref[...]
    xn_ref[...] = _rmsnorm_val(x, gn_ref[...]).astype(BF16)
    acc_ref[...] = jnp.zeros_like(acc_ref)
    if conv_shift == 1:
        xn_prev = _rmsnorm_val(prev_ref[...], gn_ref[...]).astype(BF16)
        has_prev = (i % tiles_per_seq) != 0
        row = lax.broadcasted_iota(jnp.int32, (tm, FFN_CHUNK), 0)

    def chunk_body(c, carry):
        off = pl.multiple_of(c * FFN_CHUNK, FFN_CHUNK)
        wg = wg_ref[:, pl.ds(off, FFN_CHUNK)]
        g = jnp.dot(xn_ref[...], wg, preferred_element_type=F32)
        u = jnp.dot(xn_ref[...], wu_ref[:, pl.ds(off, FFN_CHUNK)], preferred_element_type=F32)
        if conv_shift == 1:
            gp = jnp.dot(xn_prev, wg, preferred_element_type=F32)
            gp = jnp.where(has_prev, gp, 0.0)
            last, last2 = gp[SUBLANES - 1:SUBLANES, :], gp[SUBLANES - 2:SUBLANES - 1, :]
            g1 = jnp.where(row == 0, last, pltpu.roll(g, 1, 0))
            g2 = jnp.where(row == 0, last2, jnp.where(row == 1, last, pltpu.roll(g, 2, 0)))
            tail_ref[0, :, pl.ds(off, FFN_CHUNK)] = g[tm - SUBLANES:, :]
        else:
            ext = jnp.concatenate([prev_ref[:, pl.ds(off, FFN_CHUNK)], g], axis=0)
            g2 = ext[:tm, :]
            g1 = ext[conv_shift:conv_shift + tm, :]
            tail_ref[:, pl.ds(off, FFN_CHUNK)] = ext[tm:, :]
        cw = cw_ref[:, pl.ds(off, FFN_CHUNK)]
        conv = cb_ref[:, pl.ds(off, FFN_CHUNK)] + ((cw[0:1, :] * g2 + cw[1:2, :] * g1) + cw[2:3, :] * g)
        act = (conv * (1.0 / (1.0 + jnp.exp(-conv)))) * u
        acc_ref[...] += jnp.dot(act.astype(BF16), wd_ref[pl.ds(off, FFN_CHUNK), :], preferred_element_type=F32)
        return carry

    lax.fori_loop(0, n_chunks, chunk_body, 0)
    h_out = x + acc_ref[...]
    out_ref[...] = h_out
    if final:
        y_ref[...] = _rmsnorm_val(h_out, gf_ref[...])


def _conv_ffn(h, prev, gain, wg, wu, wd, conv_w, conv_b, final_gain, *, tm, conv_shift, seq_len):
    t, d = h.shape
    d_ff = wg.shape[1]
    final = final_gain is not None
    const2 = lambda i: (0, 0)
    in_specs = [pl.BlockSpec((tm, d), lambda i: (i, 0))]
    args = [h]
    if conv_shift == 1:
        rows8 = tm // SUBLANES
        in_specs.append(pl.BlockSpec((SUBLANES, d), lambda i: (jnp.maximum(i * rows8 - 1, 0), 0)))
        args.append(h)
        tail_shape = jax.ShapeDtypeStruct((t // tm, SUBLANES, d_ff), F32)
        tail_spec = pl.BlockSpec((1, SUBLANES, d_ff), lambda i: (i, 0, 0))
        tiles_per_seq = seq_len // tm
    else:
        assert t == tm
        n_prev = (CONV_WIDTH - 1) * conv_shift
        in_specs.append(pl.BlockSpec((n_prev, d_ff), const2))
        args.append(prev)
        tail_shape = jax.ShapeDtypeStruct((n_prev, d_ff), F32)
        tail_spec = pl.BlockSpec((n_prev, d_ff), const2)
        tiles_per_seq = 1
    resident = pl.Buffered(1)
    in_specs += [
        pl.BlockSpec((1, d), const2),
        pl.BlockSpec((d, d_ff), const2, pipeline_mode=resident),
        pl.BlockSpec((d, d_ff), const2, pipeline_mode=resident),
        pl.BlockSpec((d_ff, d), const2, pipeline_mode=resident),
        pl.BlockSpec((CONV_WIDTH, d_ff), const2), pl.BlockSpec((1, d_ff), const2),
    ]
    args += [gain.reshape(1, d), wg, wu, wd, conv_w, conv_b.reshape(1, d_ff)]
    out_specs = [pl.BlockSpec((tm, d), lambda i: (i, 0)), tail_spec]
    out_shape = [jax.ShapeDtypeStruct((t, d), F32), tail_shape]
    if final:
        in_specs.append(pl.BlockSpec((1, d), const2))
        args.append(final_gain.reshape(1, d))
        out_specs.append(pl.BlockSpec((tm, d), lambda i: (i, 0)))
        out_shape.append(jax.ShapeDtypeStruct((t, d), F32))
    kern = functools.partial(_ffn_kernel, conv_shift=conv_shift, tiles_per_seq=tiles_per_seq, final=final)
    return pl.pallas_call(
        kern,
        grid=(t // tm,),
        in_specs=in_specs,
        out_specs=out_specs,
        out_shape=out_shape,
        scratch_shapes=[pltpu.VMEM((tm, d), BF16), pltpu.VMEM((tm, d), F32)],
        compiler_params=_cparams(("parallel",)),
        name="conv_ffn",
    )(*args)


def _block_mean_kernel(k_ref, o_ref):
    for r in range(o_ref.shape[0]):
        o_ref[r:r + 1, :] = jnp.mean(k_ref[r * MOBA_BLOCK:(r + 1) * MOBA_BLOCK, :], axis=0, keepdims=True)


def _block_mean(k):
    t, width = k.shape
    rows = SUBLANES * MOBA_BLOCK
    return pl.pallas_call(
        _block_mean_kernel,
        grid=(t // rows,),
        in_specs=[pl.BlockSpec((rows, width), lambda i: (i, 0))],
        out_specs=pl.BlockSpec((SUBLANES, width), lambda i: (i, 0)),
        out_shape=jax.ShapeDtypeStruct((t // MOBA_BLOCK, width), F32),
        compiler_params=_cparams(("parallel",)),
        name="moba_block_mean",
    )(k)


def _top_k_mask(gate, n_valid, k_sel, axis):
    nb = gate.shape[axis]
    blk_id = lax.broadcasted_iota(jnp.int32, gate.shape, axis)
    sel = jnp.zeros(gate.shape, F32)
    g = gate
    for kk in range(k_sel):
        mx = jnp.max(g, axis=axis, keepdims=True)
        idx = jnp.min(jnp.where(g == mx, blk_id, nb), axis=axis, keepdims=True)
        hit = blk_id == idx
        counts = jnp.where(kk < n_valid, 1.0, 0.0)
        sel = jnp.maximum(sel, jnp.where(hit, counts, 0.0))
        g = jnp.where(hit, -jnp.inf, g)
    return sel


def _moba_prompt_kernel(qi_tab, kj_tab, q_ref, k_ref, vt_ref, km_ref, o_ref,
                        qbd_ref, sel_ref, m_ref, l_ref, a_ref, acc_ref, s_ref, p_ref, *, n_pairs, k_sel):
    t = pl.program_id(1)
    qi = qi_tab[t]
    kj = kj_tab[t]
    blk = MOBA_BLOCK
    nb = km_ref.shape[1]
    is_own = kj == qi

    pair_w = HEADS_PER_VREG * blk

    @pl.when(kj == 0)
    def _():
        m_ref[...] = jnp.full(m_ref.shape, NEG_BIG, F32)
        l_ref[...] = jnp.zeros_like(l_ref)
        acc_ref[...] = jnp.zeros_like(acc_ref)
        blk_id = lax.broadcasted_iota(jnp.int32, (nb, pair_w), 0)
        feat = lax.broadcasted_iota(jnp.int32, (LANES, blk), 0)
        for p in range(n_pairs):
            q_t = q_ref[:, p * LANES:(p + 1) * LANES].T
            q_bd = jnp.concatenate([jnp.where(feat < HEAD_DIM, q_t, 0.0),
                                    jnp.where(feat < HEAD_DIM, 0.0, q_t)], axis=1)
            qbd_ref[p] = (q_bd * ATTN_SCALE).astype(BF16)
            gate = _dot_precise(km_ref[0, :, p * LANES:(p + 1) * LANES], q_bd)
            gate = jnp.where(blk_id < qi, gate, -jnp.inf)
            sel_ref[p] = _top_k_mask(gate, qi, k_sel, 0)

    for p in range(n_pairs):
        s_ref[p] = jnp.dot(k_ref[:, p * LANES:(p + 1) * LANES].astype(BF16), qbd_ref[p],
                           preferred_element_type=F32)

    own = jnp.where(is_own, 1.0, 0.0)
    key = lax.broadcasted_iota(jnp.int32, (blk, pair_w), 0)
    qry = lax.broadcasted_iota(jnp.int32, (blk, pair_w), 1) & (blk - 1)
    own_bias = jnp.where(key > qry, MASK_BIAS, 0.0) * own
    row_bias = jnp.where(sel_ref[:, pl.ds(kj, 1), :] > 0.5, 0.0, MASK_BIAS) * (1.0 - own)
    s = (s_ref[...] + own_bias[None]) + row_bias
    m_old = m_ref[...]
    m_new = jnp.maximum(m_old, jnp.max(s, axis=1, keepdims=True))
    alpha = jnp.exp(m_old - m_new)
    prob = jnp.exp(s - m_new)
    l_ref[...] = alpha * l_ref[...] + jnp.sum(prob, axis=1, keepdims=True)
    m_ref[...] = m_new
    a_ref[...] = alpha
    p_ref[...] = prob.astype(BF16)
    for h in range(n_pairs * HEADS_PER_VREG):
        p, c = divmod(h, HEADS_PER_VREG)
        qs = slice(c * blk, (c + 1) * blk)
        acc_ref[h] = a_ref[p, :, qs] * acc_ref[h] + jnp.dot(vt_ref[0, h].astype(BF16), p_ref[p, :, qs],
                                                             preferred_element_type=F32)

    @pl.when(is_own)
    def _():
        for p in range(n_pairs):
            o_t = jnp.concatenate(
                [acc_ref[p * HEADS_PER_VREG + c] / l_ref[p, :, c * blk:(c + 1) * blk] for c in range(HEADS_PER_VREG)],
                axis=0)
            o_ref[:, p * LANES:(p + 1) * LANES] = o_t.T


def _moba_prompt(q, k, v_t, kmean, n_seq, seq_len, width):
    nb = seq_len // MOBA_BLOCK
    n_heads = width // HEAD_DIM
    pairs = [(a, b) for a in range(nb) for b in range(a + 1)]
    qi_tab = jnp.asarray([p[0] for p in pairs], jnp.int32)
    kj_tab = jnp.asarray([p[1] for p in pairs], jnp.int32)
    n_pairs = width // LANES
    pair_w = HEADS_PER_VREG * MOBA_BLOCK
    kern = functools.partial(_moba_prompt_kernel, n_pairs=n_pairs, k_sel=min(MOBA_TOP_K, nb))
    grid_spec = pltpu.PrefetchScalarGridSpec(
        num_scalar_prefetch=2,
        grid=(n_seq, len(pairs)),
        in_specs=[
            pl.BlockSpec((MOBA_BLOCK, width), lambda b, t, qt, kt: (b * nb + qt[t], 0)),
            pl.BlockSpec((MOBA_BLOCK, width), lambda b, t, qt, kt: (b * nb + kt[t], 0)),
            pl.BlockSpec((1, n_heads, HEAD_DIM, MOBA_BLOCK), lambda b, t, qt, kt: (b, 0, 0, kt[t])),
            pl.BlockSpec((1, nb, width), lambda b, t, qt, kt: (b, 0, 0)),
        ],
        out_specs=pl.BlockSpec((MOBA_BLOCK, width), lambda b, t, qt, kt: (b * nb + qt[t], 0)),
        scratch_shapes=[
            pltpu.VMEM((n_pairs, LANES, pair_w), BF16),
            pltpu.VMEM((n_pairs, nb, pair_w), F32),
            pltpu.VMEM((n_pairs, 1, pair_w), F32),
            pltpu.VMEM((n_pairs, 1, pair_w), F32),
            pltpu.VMEM((n_pairs, 1, pair_w), F32),
            pltpu.VMEM((n_heads, HEAD_DIM, MOBA_BLOCK), F32),
            pltpu.VMEM((n_pairs, MOBA_BLOCK, pair_w), F32),
            pltpu.VMEM((n_pairs, MOBA_BLOCK, pair_w), BF16),
        ],
    )
    return pl.pallas_call(
        kern,
        grid_spec=grid_spec,
        out_shape=jax.ShapeDtypeStruct((n_seq * seq_len, width), F32),
        compiler_params=_cparams(("parallel", "arbitrary")),
        name="moba_prompt",
    )(qi_tab, kj_tab, q, k, v_t, kmean)


SAMPLE_STEP_BYTES = 2 * 1024 * 1024
NEW_ROWS_PER_LANE_TILE = 32


def _sample_window_kernel(q_ref, kn_ref, vn_ref, knt_ref, vnt_ref, kc_ref, vc_ref,
                          ko_ref, vo_ref, o_ref, l_ref, *, window, dil, n_new):
    n = pl.program_id(0)
    hc, hd, n_buf = kc_ref.shape[2:]
    qp = q_ref.shape[3]
    shift = (LANES - n_new - (n % NEW_ROWS_PER_LANE_TILE) * n_new) % LANES
    lane = lax.broadcasted_iota(jnp.int32, (hc * hd, LANES), 1)
    is_new_lane = lane >= LANES - n_new

    for c_ref, nt_ref, out_ref in ((kc_ref, knt_ref, ko_ref), (vc_ref, vnt_ref, vo_ref)):
        rolled = pltpu.roll(c_ref[0, 0].reshape(hc * hd, n_buf), n_buf - n_new, 1)
        new_t = pltpu.roll(nt_ref[...].reshape(hc * hd, LANES), shift, 1)
        if n_buf > LANES:
            out_ref[0, 0, :, :, :n_buf - LANES] = rolled[:, :n_buf - LANES].reshape(hc, hd, n_buf - LANES)
        last = jnp.where(is_new_lane, new_t, rolled[:, n_buf - LANES:])
        out_ref[0, 0, :, :, n_buf - LANES:] = last.reshape(hc, hd, LANES)

    q_idx = lax.broadcasted_iota(jnp.int32, (qp, n_buf), 0)
    r_idx = lax.broadcasted_iota(jnp.int32, (qp, n_buf), 1)
    dist_c = n_buf + q_idx - r_idx
    dil_mask = dil - 1
    valid_c = ((dist_c & dil_mask) == 0) & (dist_c <= window)
    qn_idx = lax.broadcasted_iota(jnp.int32, (qp, qp), 0)
    jn_idx = lax.broadcasted_iota(jnp.int32, (qp, qp), 1)
    dist_n = qn_idx - jn_idx
    valid_n = (dist_n >= 0) & ((dist_n & dil_mask) == 0) & (dist_n <= window) & ((jn_idx < n_new) | (jn_idx == qn_idx))

    q = q_ref[0, 0].astype(BF16)
    s_c = _bdot(q, kc_ref[0, 0].astype(BF16), 2, 1) * ATTN_SCALE
    s_c = jnp.where(valid_c, s_c, -jnp.inf)
    s_n = _bdot(q, kn_ref[0, 0].astype(BF16), 2, 2) * ATTN_SCALE
    s_n = jnp.where(valid_n, s_n, -jnp.inf)
    m = jnp.maximum(jnp.max(s_c, axis=-1, keepdims=True), jnp.max(s_n, axis=-1, keepdims=True))
    e_c = jnp.exp(s_c - m)
    e_n = jnp.exp(s_n - m)
    den = jnp.sum(e_c, axis=-1, keepdims=True) + jnp.sum(e_n, axis=-1, keepdims=True)
    o = _bdot(e_c.astype(BF16), vc_ref[0, 0].astype(BF16), 2, 2)
    o = o + _bdot(e_n.astype(BF16), vn_ref[0, 0].astype(BF16), 2, 1)
    o_ref[0] = o / den
    l_ref[0] = jnp.broadcast_to(m + jnp.log(den), o.shape)


def _sample_window_attention(qkv_hm, qkv_t3, cache_k, cache_v, group, n_new):
    window, dil = DIL_GROUPS[group]
    _, n_seq, n_buf, n_heads, hd = cache_k.shape
    qp = qkv_hm.shape[3]
    hc = max(1, min(n_heads, SAMPLE_STEP_BYTES // (hd * n_buf * 4)))
    chunks = n_heads // hc
    ck = jnp.transpose(cache_k, (0, 1, 3, 4, 2))
    cv = jnp.transpose(cache_v, (0, 1, 3, 4, 2))
    base = group * 3

    def hm_spec(which):
        return pl.BlockSpec((1, 1, hc, qp, hd), lambda n, c: (base + which, n, c, 0, 0))

    def t_spec(which):
        return pl.BlockSpec((hc, hd, LANES), lambda n, c: ((base + which) * chunks + c, 0, n // NEW_ROWS_PER_LANE_TILE))

    cache_spec = pl.BlockSpec((1, 1, hc, hd, n_buf), lambda n, c: (0, n, c, 0, 0))
    out_spec = pl.BlockSpec((1, hc, qp, hd), lambda n, c: (n, c, 0, 0))
    out_sds = jax.ShapeDtypeStruct((n_seq, n_heads, qp, hd), F32)
    kern = functools.partial(_sample_window_kernel, window=window, dil=dil, n_new=n_new)
    ko, vo, o, lse = pl.pallas_call(
        kern,
        grid=(n_seq, chunks),
        in_specs=[hm_spec(0), hm_spec(1), hm_spec(2), t_spec(1), t_spec(2), cache_spec, cache_spec],
        out_specs=[cache_spec, cache_spec, out_spec, out_spec],
        out_shape=[jax.ShapeDtypeStruct(ck.shape, F32), jax.ShapeDtypeStruct(cv.shape, F32), out_sds, out_sds],
        compiler_params=_cparams(("parallel", "parallel")),
        name="sample_window_attn",
    )(qkv_hm, qkv_hm, qkv_hm, qkv_t3, qkv_t3, ck, cv)
    back = (0, 1, 4, 2, 3)
    return o, lse, jnp.transpose(ko, back), jnp.transpose(vo, back)


def _moba_sample_kernel(pt_ref, q_ref, kn_ref, vn_ref, k0_ref, k1_ref, v0_ref, v1_ref, o_ref,
                        m_ref, l_ref, g_ref, acc_ref, *, n_new, k_sel):
    b = pl.program_id(1)
    nb = pl.num_programs(1)
    n_heads, qp, _ = q_ref.shape[1:]
    q = q_ref[0].astype(BF16)
    k_t = jnp.concatenate([k0_ref[0], k1_ref[0]], axis=2).astype(BF16)
    v_t = jnp.concatenate([v0_ref[0], v1_ref[0]], axis=2).astype(BF16)
    raw = _bdot(q, k_t, 2, 1)
    g_ref[b] = jnp.sum(raw, axis=-1, keepdims=True) / MOBA_BLOCK
    s = raw * ATTN_SCALE
    m = jnp.max(s, axis=-1, keepdims=True)
    e = jnp.exp(s - m)
    m_ref[b] = m
    l_ref[b] = jnp.sum(e, axis=-1, keepdims=True)
    acc_ref[b] = _bdot(e.astype(BF16), v_t, 2, 2)

    @pl.when(b == nb - 1)
    def _():
        n_blocks = m_ref.shape[0]
        qn_idx = lax.broadcasted_iota(jnp.int32, (qp, qp), 0)
        jn_idx = lax.broadcasted_iota(jnp.int32, (qp, qp), 1)
        valid_own = (jn_idx <= qn_idx) & ((jn_idx < n_new) | (jn_idx == qn_idx))
        blk_lane = lax.broadcasted_iota(jnp.int32, (n_heads, qp, LANES), 2)
        gate = jnp.full((n_heads, qp, LANES), -jnp.inf, F32)
        m_blk = jnp.full((n_heads, qp, LANES), NEG_BIG, F32)
        l_blk = jnp.zeros((n_heads, qp, LANES), F32)
        for bb in range(n_blocks):
            gate = jnp.where(blk_lane == bb, g_ref[bb], gate)
            m_blk = jnp.where(blk_lane == bb, m_ref[bb], m_blk)
            l_blk = jnp.where(blk_lane == bb, l_ref[bb], l_blk)
        picked = _top_k_mask(gate, n_blocks, k_sel, 2) > 0.5
        s_own = jnp.where(valid_own, _bdot(q, kn_ref[0].astype(BF16), 2, 2) * ATTN_SCALE, -jnp.inf)
        m_all = jnp.maximum(jnp.max(s_own, axis=-1, keepdims=True),
                            jnp.max(jnp.where(picked, m_blk, NEG_BIG), axis=-1, keepdims=True))
        e_own = jnp.exp(s_own - m_all)
        w_blk = jnp.where(picked, jnp.exp(m_blk - m_all), 0.0)
        den = jnp.sum(e_own, axis=-1, keepdims=True) + jnp.sum(w_blk * l_blk, axis=-1, keepdims=True)
        num = _bdot(e_own.astype(BF16), vn_ref[0].astype(BF16), 2, 1)
        for bb in range(n_blocks):
            w = jnp.sum(jnp.where(blk_lane == bb, w_blk, 0.0), axis=-1, keepdims=True)
            num = num + w * acc_ref[bb]
        o_ref[0] = num / den


def _moba_sample(q_hm, k_hm, v_hm, cache_k, cache_v, page_table, n_new):
    n_seq, n_pages = page_table.shape
    _, n_heads, qp, hd = q_hm.shape
    pages_per_block = MOBA_BLOCK // PAGE_SIZE
    assert pages_per_block == 2 and n_pages % pages_per_block == 0 and n_new <= SAMPLE_Q_PAD
    n_blocks = n_pages // pages_per_block
    ck = jnp.transpose(cache_k, (0, 2, 3, 1))
    cv = jnp.transpose(cache_v, (0, 2, 3, 1))
    pt = page_table.reshape(-1).astype(jnp.int32)

    def page_spec(which):
        return pl.BlockSpec((1, n_heads, hd, PAGE_SIZE),
                            lambda n, b, pt_ref: (pt_ref[n * n_pages + b * pages_per_block + which], 0, 0, 0))

    seq_spec = pl.BlockSpec((1, n_heads, qp, hd), lambda n, b, pt_ref: (n, 0, 0, 0))
    kern = functools.partial(_moba_sample_kernel, n_new=n_new, k_sel=min(MOBA_TOP_K, n_blocks + 1))
    grid_spec = pltpu.PrefetchScalarGridSpec(
        num_scalar_prefetch=1,
        grid=(n_seq, n_blocks),
        in_specs=[seq_spec, seq_spec, seq_spec, page_spec(0), page_spec(1), page_spec(0), page_spec(1)],
        out_specs=seq_spec,
        scratch_shapes=[
            pltpu.VMEM((n_blocks, n_heads, qp, 1), F32),
            pltpu.VMEM((n_blocks, n_heads, qp, 1), F32),
            pltpu.VMEM((n_blocks, n_heads, qp, 1), F32),
            pltpu.VMEM((n_blocks, n_heads, qp, hd), F32),
        ],
    )
    return pl.pallas_call(
        kern,
        grid_spec=grid_spec,
        out_shape=jax.ShapeDtypeStruct((n_seq, n_heads, qp, hd), F32),
        compiler_params=_cparams(("parallel", "arbitrary")),
        name="moba_sample",
    )(pt, q_hm, k_hm, v_hm, ck, ck, cv, cv)


def _row_tile(t, cap):
    tm = min(t, cap)
    assert t % tm == 0
    return tm


def kernel(x_prompt, x_sample, cache_a_k0, cache_a_v0, cache_a_k1, cache_a_v1, cache_a_k2, cache_a_v2, cache_b_k, cache_b_v, state_ffn_conv, page_table, norm_mix, norm_ffn, norm_kv, norm_final, w_qkv_a, w_o_a, w_kv_b, w_q_b, w_o_b, w_ffn_gate, w_ffn_up, ffn_conv_w, ffn_conv_b, w_ffn_down):
    n_p, seq, d = x_prompt.shape
    n_s, dec_seq, _ = x_sample.shape
    n_groups = len(DIL_GROUPS)
    width = w_o_a.shape[1]
    n_heads = width // HEAD_DIM
    d_ff = w_ffn_gate.shape[2]
    past_len = page_table.shape[1] * PAGE_SIZE
    assert w_qkv_a.shape[0] == 1 and w_q_b.shape[0] == 1, "one self-decoder and one cross-decoder layer"
    assert width == 1024 and seq % DIL_CHUNK_ROWS == 0 and seq % (SUBLANES * MOBA_BLOCK) == 0
    assert (n_s * dec_seq) % LANES == 0 and n_s % NEW_ROWS_PER_LANE_TILE == 0
    assert NEW_ROWS_PER_LANE_TILE * dec_seq == LANES and d_ff % FFN_CHUNK == 0
    assert all(dil & (dil - 1) == 0 and window // dil == BAND_BLOCK and BAND_BLOCK * dil <= DIL_CHUNK_ROWS
               for window, dil in DIL_GROUPS)

    wqkv = w_qkv_a[0].astype(BF16)
    wqkv_t = w_qkv_a[0].T.astype(BF16)
    wo_a = w_o_a[0].astype(BF16)
    wkv = w_kv_b.astype(BF16)
    wkv_t = w_kv_b.T.astype(BF16)
    wq_b = w_q_b[0].astype(BF16)
    wo_b = w_o_b[0].astype(BF16)
    wg = w_ffn_gate.astype(BF16)
    wu = w_ffn_up.astype(BF16)
    wd = w_ffn_down.astype(BF16)

    t_p = n_p * seq
    xp = x_prompt.reshape(t_p, d)
    tab_p, tab_p_fm = _rope_tables(jnp.tile(jnp.arange(seq, dtype=jnp.int32), n_p))
    tm_mm = _row_tile(t_p, 1024)
    tm_p = _row_tile(seq, 512)

    qkv = _norm_matmul(xp, norm_mix[0], wqkv, tab_p, rope_period=3, rope_count=2, tm=tm_mm)
    outs, lses, pa = [], [], []
    for g, (window, _) in enumerate(DIL_GROUPS):
        o, lse = _dilated_attention_prompt(qkv, g, n_p, seq, width, n_groups)
        outs.append(o)
        lses.append(lse)
        keep = min(window, seq)
        k_t, v_t = _proj_feature_major(xp, norm_mix[0], wqkv_t, g * 3 + 1, tab_p_fm, n_p, seq, keep,
                                       tr=min(keep, 512))
        pa.append(jnp.transpose(k_t[None], (0, 1, 4, 2, 3)))
        pa.append(jnp.transpose(v_t[None], (0, 1, 4, 2, 3)))
    h = _merge_oproj(xp, wo_a, outs, lses, tm=tm_p)
    h, tail0 = _conv_ffn(h, None, norm_ffn[0], wg[0], wu[0], wd[0], ffn_conv_w[0], ffn_conv_b[0], None,
                         tm=tm_p, conv_shift=1, seq_len=seq)
    k_tok = _norm_matmul(h, norm_kv, wkv, tab_p, rope_period=1, rope_count=1, tm=tm_mm, n_out=width)
    kb_t, vb_t = _proj_feature_major(h, norm_kv, wkv_t, 0, tab_p_fm, n_p, seq, seq, tr=512)
    kmean = _block_mean(k_tok).reshape(n_p, seq // MOBA_BLOCK, width)
    qb = _norm_matmul(h, norm_mix[1], wq_b, tab_p, rope_period=1, rope_count=1, tm=tm_mm)
    o = _moba_prompt(qb, k_tok, vb_t, kmean, n_p, seq, width)
    h = _merge_oproj(h, wo_b, [o], [], tm=tm_p)
    h, tail1, y_p = _conv_ffn(h, None, norm_ffn[1], wg[1], wu[1], wd[1], ffn_conv_w[1], ffn_conv_b[1], norm_final,
                              tm=tm_p, conv_shift=1, seq_len=seq)
    y_p = y_p.reshape(n_p, seq, d)
    pb_k = jnp.transpose(kb_t, (0, 3, 1, 2))
    pb_v = jnp.transpose(vb_t, (0, 3, 1, 2))
    tiles_per_seq = seq // tm_p

    def prompt_conv_state(tail):
        last = tail.reshape(n_p, tiles_per_seq, SUBLANES, d_ff)[:, -1]
        return last[:, SUBLANES - (CONV_WIDTH - 1):]

    p_conv = jnp.stack([prompt_conv_state(tail0), prompt_conv_state(tail1)], axis=0)

    t_s = n_s * dec_seq
    xs = jnp.transpose(x_sample, (1, 0, 2)).reshape(t_s, d)
    pos_s = past_len + jnp.repeat(jnp.arange(dec_seq, dtype=jnp.int32), n_s)
    tab_s, _ = _rope_tables(pos_s)

    def to_head_major(a):
        c = a.shape[-1] // width
        a = jnp.transpose(a.reshape(dec_seq, n_s, c, n_heads, HEAD_DIM), (2, 1, 3, 0, 4))
        return jnp.pad(a, ((0, 0), (0, 0), (0, 0), (0, SAMPLE_Q_PAD - dec_seq), (0, 0)))

    def to_time_major(a):
        return jnp.transpose(a[:, :, :dec_seq], (2, 0, 1, 3)).reshape(t_s, width)

    def conv_prev(state):
        return jnp.transpose(state, (1, 0, 2)).reshape((CONV_WIDTH - 1) * n_s, d_ff)

    def conv_next(tail):
        return jnp.transpose(tail.reshape(CONV_WIDTH - 1, n_s, d_ff), (1, 0, 2))

    qkv_s = _norm_matmul(xs, norm_mix[0], wqkv, tab_s, rope_period=3, rope_count=2, tm=t_s)
    qkv_hm = to_head_major(qkv_s)
    qkv_t3 = jnp.transpose(qkv_s.reshape(dec_seq, n_s, n_groups * 3 * n_heads, HEAD_DIM), (2, 3, 1, 0))
    qkv_t3 = qkv_t3.reshape(n_groups * 3 * n_heads, HEAD_DIM, t_s)
    caches = ((cache_a_k0, cache_a_v0), (cache_a_k1, cache_a_v1), (cache_a_k2, cache_a_v2))
    outs, lses, sa = [], [], []
    for g in range(n_groups):
        o, lse, ko, vo = _sample_window_attention(qkv_hm, qkv_t3, caches[g][0], caches[g][1], g, dec_seq)
        outs.append(to_time_major(o))
        lses.append(to_time_major(lse))
        sa += [ko, vo]
    hs = _merge_oproj(xs, wo_a, outs, lses, tm=t_s)
    hs, s_tail0 = _conv_ffn(hs, conv_prev(state_ffn_conv[0]), norm_ffn[0], wg[0], wu[0], wd[0], ffn_conv_w[0],
                            ffn_conv_b[0], None, tm=t_s, conv_shift=n_s, seq_len=dec_seq)
    kv_s = _norm_matmul(hs, norm_kv, wkv, tab_s, rope_period=2, rope_count=1, tm=t_s)
    qb_s = _norm_matmul(hs, norm_mix[1], wq_b, tab_s, rope_period=1, rope_count=1, tm=t_s)
    kv_hm = to_head_major(kv_s)
    o = _moba_sample(to_head_major(qb_s)[0], kv_hm[0], kv_hm[1], cache_b_k, cache_b_v, page_table, dec_seq)
    hs = _merge_oproj(hs, wo_b, [to_time_major(o)], [], tm=t_s)
    hs, s_tail1, y_s = _conv_ffn(hs, conv_prev(state_ffn_conv[1]), norm_ffn[1], wg[1], wu[1], wd[1], ffn_conv_w[1],
                                 ffn_conv_b[1], norm_final, tm=t_s, conv_shift=n_s, seq_len=dec_seq)
    y_s = jnp.transpose(y_s.reshape(dec_seq, n_s, d), (1, 0, 2))
    kv_s5 = jnp.transpose(kv_s.reshape(dec_seq, n_s, 2, n_heads, HEAD_DIM), (2, 1, 0, 3, 4))
    sb_k, sb_v = kv_s5[0], kv_s5[1]
    s_conv = jnp.stack([conv_next(s_tail0), conv_next(s_tail1)], axis=0)

    return (y_p, y_s, *pa, *sa, pb_k, pb_v, sb_k, sb_v, p_conv, s_conv)
```

```python
import functools

import jax
import jax.numpy as jnp
from jax import lax
from jax.experimental import pallas as pl
from jax.experimental.pallas import tpu as pltpu

F32 = jnp.float32
BF16 = jnp.bfloat16

HEAD_DIM = 64
DIL_GROUPS = ((128, 1), (512, 4), (2048, 16))
BAND_BLOCK = 128
MOBA_BLOCK = 256
MOBA_TOP_K = 3
PAGE_SIZE = 128
CONV_WIDTH = 3
ROPE_THETA = 10000.0
NORM_EPS = 1e-6
ATTN_SCALE = HEAD_DIM ** -0.5

LANES = 128
SUBLANES = 8
VMEM_LIMIT_BYTES = 56 * 1024 * 1024

HEADS_PER_VREG = LANES // HEAD_DIM
SAMPLE_Q_PAD = SUBLANES
NEG_BIG = -1e30
MASK_BIAS = -2e30


def _cparams(semantics):
    return pltpu.CompilerParams(dimension_semantics=semantics, vmem_limit_bytes=VMEM_LIMIT_BYTES)


def _nt_dot(a, b):
    return lax.dot_general(a, b, (((1,), (1,)), ((), ())), preferred_element_type=F32)


def _bdot(a, b, ca, cb):
    return lax.dot_general(a, b, (((ca,), (cb,)), ((0,), (0,))), preferred_element_type=F32)


def _split_bf16(x):
    hi = x.astype(BF16)
    lo = (x - hi.astype(F32)).astype(BF16)
    return hi, lo


def _dot_precise(a, b):
    a_hi, a_lo = _split_bf16(a)
    b_hi, b_lo = _split_bf16(b)
    dot = functools.partial(jnp.dot, preferred_element_type=F32)
    return dot(a_hi, b_hi) + (dot(a_hi, b_lo) + dot(a_lo, b_hi))


def _rmsnorm_val(x, gain):
    r = lax.rsqrt(jnp.mean(x * x, axis=-1, keepdims=True) + NORM_EPS)
    return (x * r) * gain


def _rope_tables(pos):
    half = HEAD_DIM // 2
    inv_freq = ROPE_THETA ** (-jnp.arange(half, dtype=F32) / half)
    ang = pos.astype(F32)[:, None] * inv_freq[None, :]
    cos, sin = jnp.cos(ang), jnp.sin(ang)
    zero = jnp.zeros_like(sin)
    reps = LANES // HEAD_DIM
    cos_t = jnp.tile(jnp.concatenate([cos, cos], axis=1), (1, reps))
    sin_lo = jnp.tile(jnp.concatenate([-sin, zero], axis=1), (1, reps))
    sin_hi = jnp.tile(jnp.concatenate([zero, sin], axis=1), (1, reps))
    return (cos_t, sin_lo, sin_hi), (cos.T, sin.T)


NORM_MM_CHUNK = 256


def _norm_mm_kernel(x_ref, g_ref, w_ref, cos_ref, slo_ref, shi_ref, o_ref, xn_ref, *, rope_period, rope_count):
    j = pl.program_id(1)

    @pl.when(j == 0)
    def _():
        xn_ref[...] = _rmsnorm_val(x_ref[...], g_ref[...]).astype(BF16)

    tn = o_ref.shape[1]
    half = HEAD_DIM // 2

    def chunks():
        for c in range(tn // NORM_MM_CHUNK):
            cols = slice(c * NORM_MM_CHUNK, (c + 1) * NORM_MM_CHUNK)
            yield c, jnp.dot(xn_ref[...], w_ref[:, cols], preferred_element_type=F32)

    def store_rope():
        cos, slo, shi = cos_ref[...], slo_ref[...], shi_ref[...]
        for c, y in chunks():
            for cc in range(NORM_MM_CHUNK // LANES):
                yc = y[:, cc * LANES:(cc + 1) * LANES]
                lo = c * NORM_MM_CHUNK + cc * LANES
                o_ref[:, lo:lo + LANES] = (
                    yc * cos + pltpu.roll(yc, LANES - half, 1) * slo + pltpu.roll(yc, half, 1) * shi)

    def store_plain():
        for c, y in chunks():
            o_ref[:, c * NORM_MM_CHUNK:(c + 1) * NORM_MM_CHUNK] = y

    if rope_count == rope_period:
        store_rope()
    else:
        is_rope = (j % rope_period) < rope_count
        pl.when(is_rope)(store_rope)
        pl.when(jnp.logical_not(is_rope))(store_plain)


def _norm_matmul(x, gain, w_bf16, tables, *, rope_period, rope_count, tm, n_out=None, tn=1024):
    t, d = x.shape
    n = w_bf16.shape[1] if n_out is None else n_out
    kern = functools.partial(_norm_mm_kernel, rope_period=rope_period, rope_count=rope_count)
    tab_spec = pl.BlockSpec((tm, LANES), lambda i, j: (i, 0))
    return pl.pallas_call(
        kern,
        grid=(t // tm, n // tn),
        in_specs=[
            pl.BlockSpec((tm, d), lambda i, j: (i, 0)),
            pl.BlockSpec((1, d), lambda i, j: (0, 0)),
            pl.BlockSpec((d, tn), lambda i, j: (0, j)),
            tab_spec, tab_spec, tab_spec,
        ],
        out_specs=pl.BlockSpec((tm, tn), lambda i, j: (i, j)),
        out_shape=jax.ShapeDtypeStruct((t, n), F32),
        scratch_shapes=[pltpu.VMEM((tm, d), BF16)],
        compiler_params=_cparams(("parallel", "arbitrary")),
        name="norm_matmul",
    )(x, gain.reshape(1, d), w_bf16, *tables)


def _proj_fm_kernel(x_ref, g_ref, wt_ref, cos_ref, sin_ref, k_ref, v_ref, xn_ref):
    j = pl.program_id(1)
    half = HEAD_DIM // 2

    @pl.when(j == 0)
    def _():
        xn_ref[...] = _rmsnorm_val(x_ref[...], g_ref[...]).astype(BF16)

    def project():
        y = _nt_dot(wt_ref[...], xn_ref[...])
        return y.reshape(y.shape[0] // HEAD_DIM, HEAD_DIM, y.shape[1])

    @pl.when(j == 0)
    def _():
        y = project()
        cos, sin = cos_ref[...], sin_ref[...]
        x1, x2 = y[:, :half, :], y[:, half:, :]
        k_ref[0] = jnp.concatenate([x1 * cos - x2 * sin, x2 * cos + x1 * sin], axis=1)

    @pl.when(j == 1)
    def _():
        v_ref[0] = project()


def _proj_feature_major(x, gain, wt_bf16, w_block, tables_fm, n_seq, seq_len, keep, *, tr):
    d = x.shape[1]
    width = 1024
    n_heads = width // HEAD_DIM
    tiles = keep // tr
    seq_tiles = seq_len // tr
    first = seq_tiles - tiles

    def rows(i):
        return (i // tiles) * seq_tiles + first + (i % tiles)

    out_spec = pl.BlockSpec((1, n_heads, HEAD_DIM, tr), lambda i, j: (i // tiles, 0, 0, i % tiles))
    out_sds = jax.ShapeDtypeStruct((n_seq, n_heads, HEAD_DIM, keep), F32)
    tab_spec = pl.BlockSpec((HEAD_DIM // 2, tr), lambda i, j: (0, rows(i)))
    return pl.pallas_call(
        _proj_fm_kernel,
        grid=(n_seq * tiles, 2),
        in_specs=[
            pl.BlockSpec((tr, d), lambda i, j: (rows(i), 0)),
            pl.BlockSpec((1, d), lambda i, j: (0, 0)),
            pl.BlockSpec((width, d), lambda i, j: (w_block + j, 0)),
            tab_spec, tab_spec,
        ],
        out_specs=[out_spec, out_spec],
        out_shape=[out_sds, out_sds],
        scratch_shapes=[pltpu.VMEM((tr, d), BF16)],
        compiler_params=_cparams(("parallel", "arbitrary")),
        name="proj_feature_major",
    )(x, gain.reshape(1, d), wt_bf16, *tables_fm)


DIL_CHUNK_ROWS = 2048
DIL_HEAD_LANES = LANES


def _dil_attn_kernel(q_ref, k_ref, v_ref, kp_ref, vp_ref, o_ref, l_ref, *, dil, n_back):
    i = pl.program_id(1)
    blk = BAND_BLOCK
    span = blk * dil
    n_units = q_ref.shape[0] // blk
    pair_w = HEADS_PER_VREG * blk
    ki = lax.broadcasted_iota(jnp.int32, (2 * blk, pair_w), 0)
    qi = (lax.broadcasted_iota(jnp.int32, (2 * blk, pair_w), 1) & (blk - 1)) + blk
    dist = qi - ki
    in_band = (dist >= 0) & (dist <= n_back)
    is_cur = ki >= blk
    first_head = lax.broadcasted_iota(jnp.int32, (blk, LANES), 1) < HEAD_DIM

    def rows(ref, start):
        if dil == 1:
            return ref[pl.ds(pl.multiple_of(start, blk), blk), :]
        return ref[pl.ds(start, blk, stride=dil), :]

    def unit_body(u, carry):
        sb = u // dil
        r = u % dil
        start = sb * span + r
        prev_here = sb > 0
        prev_start = jnp.maximum(start - span, 0)
        q2 = rows(q_ref, start) * ATTN_SCALE
        q_bd = jnp.concatenate([jnp.where(first_head, q2, 0.0), jnp.where(first_head, 0.0, q2)], axis=0).astype(BF16)
        k_prev = jnp.where(prev_here, rows(k_ref, prev_start), rows(kp_ref, r))
        v_prev = jnp.where(prev_here, rows(v_ref, prev_start), rows(vp_ref, r))
        k2 = jnp.concatenate([k_prev, rows(k_ref, start)], axis=0).astype(BF16)
        v2 = jnp.concatenate([v_prev, rows(v_ref, start)], axis=0).astype(BF16)
        mask = in_band & (is_cur | (prev_here | (i > 0)))
        s = jnp.where(mask, _nt_dot(k2, q_bd), -jnp.inf)
        m = jnp.max(s, axis=0, keepdims=True)
        e = jnp.exp(s - m)
        den = jnp.sum(e, axis=0, keepdims=True)
        prob = (e * (1.0 / den)).astype(BF16)
        lse = m + jnp.log(den)
        outs, lses = [], []
        for h in range(HEADS_PER_VREG):
            outs.append(lax.dot_general(prob[:, h * blk:(h + 1) * blk], v2[:, h * HEAD_DIM:(h + 1) * HEAD_DIM],
                                        (((0,), (0,)), ((), ())), preferred_element_type=F32))
            lses.append(jnp.broadcast_to(lse[:, h * blk:(h + 1) * blk], (HEAD_DIM, blk)))
        o_val = jnp.concatenate(outs, axis=1)
        l_val = jnp.concatenate(lses, axis=0).T
        if dil == 1:
            o_ref[pl.ds(pl.multiple_of(start, blk), blk), :] = o_val
            l_ref[pl.ds(pl.multiple_of(start, blk), blk), :] = l_val
        else:
            o_ref[pl.ds(start, blk, stride=dil), :] = o_val
            l_ref[pl.ds(start, blk, stride=dil), :] = l_val
        return carry

    lax.fori_loop(0, n_units, unit_body, 0, unroll=4)


def _dilated_attention_prompt(qkv, group, n_seq, seq_len, width, n_groups):
    window, dil = DIL_GROUPS[group]
    rows, hl = DIL_CHUNK_ROWS, DIL_HEAD_LANES
    span = BAND_BLOCK * dil
    chunks = seq_len // rows
    per_w = width // hl
    base = group * 3

    def cur(which):
        return pl.BlockSpec((rows, hl), lambda b, i, c: (b * chunks + i, (base + which) * per_w + c))

    def prev(which):
        return pl.BlockSpec(
            (span, hl),
            lambda b, i, c: (jnp.maximum((b * chunks + i) * (rows // span) - 1, 0), (base + which) * per_w + c))

    out_spec = pl.BlockSpec((rows, hl), lambda b, i, c: (b * chunks + i, c))
    out_sds = jax.ShapeDtypeStruct((n_seq * seq_len, width), F32)
    kern = functools.partial(_dil_attn_kernel, dil=dil, n_back=window // dil)
    return pl.pallas_call(
        kern,
        grid=(n_seq, chunks, per_w),
        in_specs=[cur(0), cur(1), cur(2), prev(1), prev(2)],
        out_specs=[out_spec, out_spec],
        out_shape=[out_sds, out_sds],
        compiler_params=_cparams(("parallel", "parallel", "parallel")),
        name="dilated_attn_prompt",
    )(qkv, qkv, qkv, qkv, qkv)


def _merge_oproj_kernel(*refs, n_groups):
    x_ref, w_ref = refs[0], refs[1]
    o_refs = refs[2:2 + n_groups]
    l_refs = refs[2 + n_groups:2 + 2 * n_groups]
    out_ref = refs[-1]
    if n_groups == 1:
        o = o_refs[0][...]
    else:
        ls = [r[...] for r in l_refs]
        m = functools.reduce(jnp.maximum, ls)
        es = [jnp.exp(l - m) for l in ls]
        den = functools.reduce(lambda a, b: a + b, es)
        o = None
        for e, o_ref in zip(es, o_refs):
            term = (e / den) * o_ref[...]
            o = term if o is None else o + term
    out_ref[...] = x_ref[...] + jnp.dot(o.astype(BF16), w_ref[...], preferred_element_type=F32)


def _merge_oproj(x, w_bf16, outs, lses, *, tm):
    t, d = x.shape
    width = w_bf16.shape[0]
    n_groups = len(outs)
    row_spec = pl.BlockSpec((tm, width), lambda i: (i, 0))
    kern = functools.partial(_merge_oproj_kernel, n_groups=n_groups)
    return pl.pallas_call(
        kern,
        grid=(t // tm,),
        in_specs=[pl.BlockSpec((tm, d), lambda i: (i, 0)), pl.BlockSpec((width, d), lambda i: (0, 0))]
        + [row_spec] * (n_groups + len(lses)),
        out_specs=pl.BlockSpec((tm, d), lambda i: (i, 0)),
        out_shape=jax.ShapeDtypeStruct((t, d), F32),
        compiler_params=_cparams(("parallel",)),
        name="merge_oproj",
    )(x, w_bf16, *outs, *lses)


FFN_CHUNK = 256


def _ffn_kernel(*refs, conv_shift, tiles_per_seq, final):
    it = iter(refs)
    h_ref, prev_ref, gn_ref, wg_ref, wu_ref, wd_ref, cw_ref, cb_ref = (next(it) for _ in range(8))
    gf_ref = next(it) if final else None
    out_ref, tail_ref = next(it), next(it)
    y_ref = next(it) if final else None
    xn_ref, acc_ref = next(it), next(it)

    i = pl.program_id(0)
    tm = h_ref.shape[0]
    n_chunks = wg_ref.shape[1] // FFN_CHUNK
    x = h_ref[...]
    xn_ref[...] = _rmsnorm_val(x, gn_ref[...]).astype(BF16)
    acc_ref[...] = jnp.zeros_like(acc_ref)
    if conv_shift == 1:
        xn_prev = _rmsnorm_val(prev_ref[...], gn_ref[...]).astype(BF16)
        has_prev = (i % tiles_per_seq) != 0
        row = lax.broadcasted_iota(jnp.int32, (tm, FFN_CHUNK), 0)

    def chunk_body(c, carry):
        off = pl.multiple_of(c * FFN_CHUNK, FFN_CHUNK)
        wg = wg_ref[:, pl.ds(off, FFN_CHUNK)]
        g = jnp.dot(xn_ref[...], wg, preferred_element_type=F32)
        u = jnp.dot(xn_ref[...], wu_ref[:, pl.ds(off, FFN_CHUNK)], preferred_element_type=F32)
        if conv_shift == 1:
            gp = jnp.dot(xn_prev, wg, preferred_element_type=F32)
            gp = jnp.where(has_prev, gp, 0.0)
            last, last2 = gp[SUBLANES - 1:SUBLANES, :], gp[SUBLANES - 2:SUBLANES - 1, :]
            g1 = jnp.where(row == 0, last, pltpu.roll(g, 1, 0))
            g2 = jnp.where(row == 0, last2, jnp.where(row == 1, last, pltpu.roll(g, 2, 0)))
            tail_ref[0, :, pl.ds(off, FFN_CHUNK)] = g[tm - SUBLANES:, :]
        else:
            ext = jnp.concatenate([prev_ref[:, pl.ds(off, FFN_CHUNK)], g], axis=0)
            g2 = ext[:tm, :]
            g1 = ext[conv_shift:conv_shift + tm, :]
            tail_ref[:, pl.ds(off, FFN_CHUNK)] = ext[tm:, :]
        cw = cw_ref[:, pl.ds(off, FFN_CHUNK)]
        conv = cb_ref[:, pl.ds(off, FFN_CHUNK)] + ((cw[0:1, :] * g2 + cw[1:2, :] * g1) + cw[2:3, :] * g)
        act = (conv * (1.0 / (1.0 + jnp.exp(-conv)))) * u
        acc_ref[...] += jnp.dot(act.astype(BF16), wd_ref[pl.ds(off, FFN_CHUNK), :], preferred_element_type=F32)
        return carry

    lax.fori_loop(0, n_chunks, chunk_body, 0, unroll=True)
    h_out = x + acc_ref[...]
    out_ref[...] = h_out
    if final:
        y_ref[...] = _rmsnorm_val(h_out, gf_ref[...])


def _conv_ffn(h, prev, gain, wg, wu, wd, conv_w, conv_b, final_gain, *, tm, conv_shift, seq_len):
    t, d = h.shape
    d_ff = wg.shape[1]
    final = final_gain is not None
    const2 = lambda i: (0, 0)
    in_specs = [pl.BlockSpec((tm, d), lambda i: (i, 0))]
    args = [h]
    if conv_shift == 1:
        rows8 = tm // SUBLANES
        in_specs.append(pl.BlockSpec((SUBLANES, d), lambda i: (jnp.maximum(i * rows8 - 1, 0), 0)))
        args.append(h)
        tail_shape = jax.ShapeDtypeStruct((t // tm, SUBLANES, d_ff), F32)
        tail_spec = pl.BlockSpec((1, SUBLANES, d_ff), lambda i: (i, 0, 0))
        tiles_per_seq = seq_len // tm
    else:
        assert t == tm
        n_prev = (CONV_WIDTH - 1) * conv_shift
        in_specs.append(pl.BlockSpec((n_prev, d_ff), const2))
        args.append(prev)
        tail_shape = jax.ShapeDtypeStruct((n_prev, d_ff), F32)
        tail_spec = pl.BlockSpec((n_prev, d_ff), const2)
        tiles_per_seq = 1
    resident = pl.Buffered(1)
    in_specs += [
        pl.BlockSpec((1, d), const2),
        pl.BlockSpec((d, d_ff), const2, pipeline_mode=resident),
        pl.BlockSpec((d, d_ff), const2, pipeline_mode=resident),
        pl.BlockSpec((d_ff, d), const2, pipeline_mode=resident),
        pl.BlockSpec((CONV_WIDTH, d_ff), const2), pl.BlockSpec((1, d_ff), const2),
    ]
    args += [gain.reshape(1, d), wg, wu, wd, conv_w, conv_b.reshape(1, d_ff)]
    out_specs = [pl.BlockSpec((tm, d), lambda i: (i, 0)), tail_spec]
    out_shape = [jax.ShapeDtypeStruct((t, d), F32), tail_shape]
    if final:
        in_specs.append(pl.BlockSpec((1, d), const2))
        args.append(final_gain.reshape(1, d))
        out_specs.append(pl.BlockSpec((tm, d), lambda i: (i, 0)))
        out_shape.append(jax.ShapeDtypeStruct((t, d), F32))
    kern = functools.partial(_ffn_kernel, conv_shift=conv_shift, tiles_per_seq=tiles_per_seq, final=final)
    return pl.pallas_call(
        kern,
        grid=(t // tm,),
        in_specs=in_specs,
        out_specs=out_specs,
        out_shape=out_shape,
        scratch_shapes=[pltpu.VMEM((tm, d), BF16), pltpu.VMEM((tm, d), F32)],
        compiler_params=_cparams(("parallel",)),
        name="conv_ffn",
    )(*args)


def _block_mean_kernel(k_ref, o_ref):
    for r in range(o_ref.shape[0]):
        o_ref[r:r + 1, :] = jnp.mean(k_ref[r * MOBA_BLOCK:(r + 1) * MOBA_BLOCK, :], axis=0, keepdims=True)


def _block_mean(k):
    t, width = k.shape
    rows = SUBLANES * MOBA_BLOCK
    return pl.pallas_call(
        _block_mean_kernel,
        grid=(t // rows,),
        in_specs=[pl.BlockSpec((rows, width), lambda i: (i, 0))],
        out_specs=pl.BlockSpec((SUBLANES, width), lambda i: (i, 0)),
        out_shape=jax.ShapeDtypeStruct((t // MOBA_BLOCK, width), F32),
        compiler_params=_cparams(("parallel",)),
        name="moba_block_mean",
    )(k)


def _top_k_mask(gate, n_valid, k_sel, axis):
    nb = gate.shape[axis]
    blk_id = lax.broadcasted_iota(jnp.int32, gate.shape, axis)
    sel = jnp.zeros(gate.shape, F32)
    g = gate
    for kk in range(k_sel):
        mx = jnp.max(g, axis=axis, keepdims=True)
        idx = jnp.min(jnp.where(g == mx, blk_id, nb), axis=axis, keepdims=True)
        hit = blk_id == idx
        counts = jnp.where(kk < n_valid, 1.0, 0.0)
        sel = jnp.maximum(sel, jnp.where(hit, counts, 0.0))
        g = jnp.where(hit, -jnp.inf, g)
    return sel


def _moba_prompt_kernel(qi_tab, kj_tab, q_ref, k_ref, vt_ref, km_ref, o_ref,
                        qbd_ref, sel_ref, m_ref, l_ref, a_ref, acc_ref, s_ref, p_ref, *, n_pairs, k_sel):
    t = pl.program_id(1)
    qi = qi_tab[t]
    kj = kj_tab[t]
    blk = MOBA_BLOCK
    nb = km_ref.shape[1]
    is_own = kj == qi

    pair_w = HEADS_PER_VREG * blk

    @pl.when(kj == 0)
    def _():
        m_ref[...] = jnp.full(m_ref.shape, NEG_BIG, F32)
        l_ref[...] = jnp.zeros_like(l_ref)
        acc_ref[...] = jnp.zeros_like(acc_ref)
        blk_id = lax.broadcasted_iota(jnp.int32, (nb, pair_w), 0)
        feat = lax.broadcasted_iota(jnp.int32, (LANES, blk), 0)
        for p in range(n_pairs):
            q_t = q_ref[:, p * LANES:(p + 1) * LANES].T
            q_bd = jnp.concatenate([jnp.where(feat < HEAD_DIM, q_t, 0.0),
                                    jnp.where(feat < HEAD_DIM, 0.0, q_t)], axis=1)
            qbd_ref[p] = (q_bd * ATTN_SCALE).astype(BF16)
            gate = _dot_precise(km_ref[0, :, p * LANES:(p + 1) * LANES], q_bd)
            gate = jnp.where(blk_id < qi, gate, -jnp.inf)
            sel_ref[p] = _top_k_mask(gate, qi, k_sel, 0)

    for p in range(n_pairs):
        s_ref[p] = jnp.dot(k_ref[:, p * LANES:(p + 1) * LANES].astype(BF16), qbd_ref[p],
                           preferred_element_type=F32)

    @pl.when(is_own)
    def _():
        key = lax.broadcasted_iota(jnp.int32, (blk, pair_w), 0)
        qry = lax.broadcasted_iota(jnp.int32, (blk, pair_w), 1) & (blk - 1)
        bias = jnp.where(key > qry, MASK_BIAS, 0.0)
        for p in range(n_pairs):
            s_ref[p] = s_ref[p] + bias

    takes_part = jnp.where(is_own, 1.0, sel_ref[:, pl.ds(kj, 1), :]) > 0.5
    s = s_ref[...]
    m_old = m_ref[...]
    m_new = jnp.where(takes_part, jnp.maximum(m_old, jnp.max(s, axis=1, keepdims=True)), m_old)
    alpha = jnp.exp(m_old - m_new)
    prob = jnp.exp(s - jnp.where(takes_part, m_new, -NEG_BIG))
    l_ref[...] = alpha * l_ref[...] + jnp.sum(prob, axis=1, keepdims=True)
    m_ref[...] = m_new
    a_ref[...] = alpha
    p_ref[...] = prob.astype(BF16)
    for h in range(n_pairs * HEADS_PER_VREG):
        p, c = divmod(h, HEADS_PER_VREG)
        qs = slice(c * blk, (c + 1) * blk)
        acc_ref[h] = a_ref[p, :, qs] * acc_ref[h] + jnp.dot(vt_ref[0, h].astype(BF16), p_ref[p, :, qs],
                                                             preferred_element_type=F32)

    @pl.when(is_own)
    def _():
        for p in range(n_pairs):
            o_t = jnp.concatenate(
                [acc_ref[p * HEADS_PER_VREG + c] / l_ref[p, :, c * blk:(c + 1) * blk] for c in range(HEADS_PER_VREG)],
                axis=0)
            o_ref[:, p * LANES:(p + 1) * LANES] = o_t.T


def _moba_prompt(q, k, v_t, kmean, n_seq, seq_len, width):
    nb = seq_len // MOBA_BLOCK
    n_heads = width // HEAD_DIM
    pairs = [(a, b) for a in range(nb) for b in range(a + 1)]
    qi_tab = jnp.asarray([p[0] for p in pairs], jnp.int32)
    kj_tab = jnp.asarray([p[1] for p in pairs], jnp.int32)
    n_pairs = width // LANES
    pair_w = HEADS_PER_VREG * MOBA_BLOCK
    kern = functools.partial(_moba_prompt_kernel, n_pairs=n_pairs, k_sel=min(MOBA_TOP_K, nb))
    grid_spec = pltpu.PrefetchScalarGridSpec(
        num_scalar_prefetch=2,
        grid=(n_seq, len(pairs)),
        in_specs=[
            pl.BlockSpec((MOBA_BLOCK, width), lambda b, t, qt, kt: (b * nb + qt[t], 0)),
            pl.BlockSpec((MOBA_BLOCK, width), lambda b, t, qt, kt: (b * nb + kt[t], 0)),
            pl.BlockSpec((1, n_heads, HEAD_DIM, MOBA_BLOCK), lambda b, t, qt, kt: (b, 0, 0, kt[t])),
            pl.BlockSpec((1, nb, width), lambda b, t, qt, kt: (b, 0, 0)),
        ],
        out_specs=pl.BlockSpec((MOBA_BLOCK, width), lambda b, t, qt, kt: (b * nb + qt[t], 0)),
        scratch_shapes=[
            pltpu.VMEM((n_pairs, LANES, pair_w), BF16),
            pltpu.VMEM((n_pairs, nb, pair_w), F32),
            pltpu.VMEM((n_pairs, 1, pair_w), F32),
            pltpu.VMEM((n_pairs, 1, pair_w), F32),
            pltpu.VMEM((n_pairs, 1, pair_w), F32),
            pltpu.VMEM((n_heads, HEAD_DIM, MOBA_BLOCK), F32),
            pltpu.VMEM((n_pairs, MOBA_BLOCK, pair_w), F32),
            pltpu.VMEM((n_pairs, MOBA_BLOCK, pair_w), BF16),
        ],
    )
    return pl.pallas_call(
        kern,
        grid_spec=grid_spec,
        out_shape=jax.ShapeDtypeStruct((n_seq * seq_len, width), F32),
        compiler_params=_cparams(("parallel", "arbitrary")),
        name="moba_prompt",
    )(qi_tab, kj_tab, q, k, v_t, kmean)


SAMPLE_STEP_BYTES = 2 * 1024 * 1024
NEW_ROWS_PER_LANE_TILE = 32


def _sample_window_kernel(q_ref, kn_ref, vn_ref, knt_ref, vnt_ref, kc_ref, vc_ref,
                          ko_ref, vo_ref, o_ref, l_ref, *, window, dil, n_new):
    n = pl.program_id(0)
    hc, hd, n_buf = kc_ref.shape[2:]
    qp = q_ref.shape[3]
    shift = (LANES - n_new - (n % NEW_ROWS_PER_LANE_TILE) * n_new) % LANES
    lane = lax.broadcasted_iota(jnp.int32, (hc * hd, LANES), 1)
    is_new_lane = lane >= LANES - n_new

    for c_ref, nt_ref, out_ref in ((kc_ref, knt_ref, ko_ref), (vc_ref, vnt_ref, vo_ref)):
        rolled = pltpu.roll(c_ref[0, 0].reshape(hc * hd, n_buf), n_buf - n_new, 1)
        new_t = pltpu.roll(nt_ref[...].reshape(hc * hd, LANES), shift, 1)
        if n_buf > LANES:
            out_ref[0, 0, :, :, :n_buf - LANES] = rolled[:, :n_buf - LANES].reshape(hc, hd, n_buf - LANES)
        last = jnp.where(is_new_lane, new_t, rolled[:, n_buf - LANES:])
        out_ref[0, 0, :, :, n_buf - LANES:] = last.reshape(hc, hd, LANES)

    q_idx = lax.broadcasted_iota(jnp.int32, (qp, n_buf), 0)
    r_idx = lax.broadcasted_iota(jnp.int32, (qp, n_buf), 1)
    dist_c = n_buf + q_idx - r_idx
    dil_mask = dil - 1
    valid_c = ((dist_c & dil_mask) == 0) & (dist_c <= window)
    qn_idx = lax.broadcasted_iota(jnp.int32, (qp, qp), 0)
    jn_idx = lax.broadcasted_iota(jnp.int32, (qp, qp), 1)
    dist_n = qn_idx - jn_idx
    valid_n = (dist_n >= 0) & ((dist_n & dil_mask) == 0) & (dist_n <= window) & ((jn_idx < n_new) | (jn_idx == qn_idx))

    q = q_ref[0, 0].astype(BF16)
    s_c = _bdot(q, kc_ref[0, 0].astype(BF16), 2, 1) * ATTN_SCALE
    s_c = jnp.where(valid_c, s_c, -jnp.inf)
    s_n = _bdot(q, kn_ref[0, 0].astype(BF16), 2, 2) * ATTN_SCALE
    s_n = jnp.where(valid_n, s_n, -jnp.inf)
    m = jnp.maximum(jnp.max(s_c, axis=-1, keepdims=True), jnp.max(s_n, axis=-1, keepdims=True))
    e_c = jnp.exp(s_c - m)
    e_n = jnp.exp(s_n - m)
    den = jnp.sum(e_c, axis=-1, keepdims=True) + jnp.sum(e_n, axis=-1, keepdims=True)
    o = _bdot(e_c.astype(BF16), vc_ref[0, 0].astype(BF16), 2, 2)
    o = o + _bdot(e_n.astype(BF16), vn_ref[0, 0].astype(BF16), 2, 1)
    o_ref[0] = o / den
    l_ref[0] = jnp.broadcast_to(m + jnp.log(den), o.shape)


def _sample_window_attention(qkv_hm, qkv_t3, cache_k, cache_v, group, n_new):
    window, dil = DIL_GROUPS[group]
    _, n_seq, n_buf, n_heads, hd = cache_k.shape
    qp = qkv_hm.shape[3]
    hc = max(1, min(n_heads, SAMPLE_STEP_BYTES // (hd * n_buf * 4)))
    chunks = n_heads // hc
    ck = jnp.transpose(cache_k, (0, 1, 3, 4, 2))
    cv = jnp.transpose(cache_v, (0, 1, 3, 4, 2))
    base = group * 3

    def hm_spec(which):
        return pl.BlockSpec((1, 1, hc, qp, hd), lambda n, c: (base + which, n, c, 0, 0))

    def t_spec(which):
        return pl.BlockSpec((hc, hd, LANES), lambda n, c: ((base + which) * chunks + c, 0, n // NEW_ROWS_PER_LANE_TILE))

    cache_spec = pl.BlockSpec((1, 1, hc, hd, n_buf), lambda n, c: (0, n, c, 0, 0))
    out_spec = pl.BlockSpec((1, hc, qp, hd), lambda n, c: (n, c, 0, 0))
    out_sds = jax.ShapeDtypeStruct((n_seq, n_heads, qp, hd), F32)
    kern = functools.partial(_sample_window_kernel, window=window, dil=dil, n_new=n_new)
    ko, vo, o, lse = pl.pallas_call(
        kern,
        grid=(n_seq, chunks),
        in_specs=[hm_spec(0), hm_spec(1), hm_spec(2), t_spec(1), t_spec(2), cache_spec, cache_spec],
        out_specs=[cache_spec, cache_spec, out_spec, out_spec],
        out_shape=[jax.ShapeDtypeStruct(ck.shape, F32), jax.ShapeDtypeStruct(cv.shape, F32), out_sds, out_sds],
        compiler_params=_cparams(("parallel", "parallel")),
        name="sample_window_attn",
    )(qkv_hm, qkv_hm, qkv_hm, qkv_t3, qkv_t3, ck, cv)
    back = (0, 1, 4, 2, 3)
    return o, lse, jnp.transpose(ko, back), jnp.transpose(vo, back)


def _moba_sample_kernel(pt_ref, q_ref, kn_ref, vn_ref, k0_ref, k1_ref, v0_ref, v1_ref, o_ref,
                        m_ref, l_ref, g_ref, acc_ref, *, n_new, k_sel):
    b = pl.program_id(1)
    nb = pl.num_programs(1)
    n_heads, qp, _ = q_ref.shape[1:]
    q = q_ref[0].astype(BF16)
    k_t = jnp.concatenate([k0_ref[0], k1_ref[0]], axis=2).astype(BF16)
    v_t = jnp.concatenate([v0_ref[0], v1_ref[0]], axis=2).astype(BF16)
    raw = _bdot(q, k_t, 2, 1)
    g_ref[b] = jnp.sum(raw, axis=-1, keepdims=True) / MOBA_BLOCK
    s = raw * ATTN_SCALE
    m = jnp.max(s, axis=-1, keepdims=True)
    e = jnp.exp(s - m)
    m_ref[b] = m
    l_ref[b] = jnp.sum(e, axis=-1, keepdims=True)
    acc_ref[b] = _bdot(e.astype(BF16), v_t, 2, 2)

    @pl.when(b == nb - 1)
    def _():
        n_blocks = m_ref.shape[0]
        qn_idx = lax.broadcasted_iota(jnp.int32, (qp, qp), 0)
        jn_idx = lax.broadcasted_iota(jnp.int32, (qp, qp), 1)
        valid_own = (jn_idx <= qn_idx) & ((jn_idx < n_new) | (jn_idx == qn_idx))
        blk_lane = lax.broadcasted_iota(jnp.int32, (n_heads, qp, LANES), 2)
        gate = jnp.full((n_heads, qp, LANES), -jnp.inf, F32)
        m_blk = jnp.full((n_heads, qp, LANES), NEG_BIG, F32)
        l_blk = jnp.zeros((n_heads, qp, LANES), F32)
        for bb in range(n_blocks):
            gate = jnp.where(blk_lane == bb, g_ref[bb], gate)
            m_blk = jnp.where(blk_lane == bb, m_ref[bb], m_blk)
            l_blk = jnp.where(blk_lane == bb, l_ref[bb], l_blk)
        picked = _top_k_mask(gate, n_blocks, k_sel, 2) > 0.5
        s_own = jnp.where(valid_own, _bdot(q, kn_ref[0].astype(BF16), 2, 2) * ATTN_SCALE, -jnp.inf)
        m_all = jnp.maximum(jnp.max(s_own, axis=-1, keepdims=True),
                            jnp.max(jnp.where(picked, m_blk, NEG_BIG), axis=-1, keepdims=True))
        e_own = jnp.exp(s_own - m_all)
        w_blk = jnp.where(picked, jnp.exp(m_blk - m_all), 0.0)
        den = jnp.sum(e_own, axis=-1, keepdims=True) + jnp.sum(w_blk * l_blk, axis=-1, keepdims=True)
        num = _bdot(e_own.astype(BF16), vn_ref[0].astype(BF16), 2, 1)
        for bb in range(n_blocks):
            w = jnp.sum(jnp.where(blk_lane == bb, w_blk, 0.0), axis=-1, keepdims=True)
            num = num + w * acc_ref[bb]
        o_ref[0] = num / den


def _moba_sample(q_hm, k_hm, v_hm, cache_k, cache_v, page_table, n_new):
    n_seq, n_pages = page_table.shape
    _, n_heads, qp, hd = q_hm.shape
    pages_per_block = MOBA_BLOCK // PAGE_SIZE
    assert pages_per_block == 2 and n_pages % pages_per_block == 0 and n_new <= SAMPLE_Q_PAD
    n_blocks = n_pages // pages_per_block
    ck = jnp.transpose(cache_k, (0, 2, 3, 1))
    cv = jnp.transpose(cache_v, (0, 2, 3, 1))
    pt = page_table.reshape(-1).astype(jnp.int32)

    def page_spec(which):
        return pl.BlockSpec((1, n_heads, hd, PAGE_SIZE),
                            lambda n, b, pt_ref: (pt_ref[n * n_pages + b * pages_per_block + which], 0, 0, 0))

    seq_spec = pl.BlockSpec((1, n_heads, qp, hd), lambda n, b, pt_ref: (n, 0, 0, 0))
    kern = functools.partial(_moba_sample_kernel, n_new=n_new, k_sel=min(MOBA_TOP_K, n_blocks + 1))
    grid_spec = pltpu.PrefetchScalarGridSpec(
        num_scalar_prefetch=1,
        grid=(n_seq, n_blocks),
        in_specs=[seq_spec, seq_spec, seq_spec, page_spec(0), page_spec(1), page_spec(0), page_spec(1)],
        out_specs=seq_spec,
        scratch_shapes=[
            pltpu.VMEM((n_blocks, n_heads, qp, 1), F32),
            pltpu.VMEM((n_blocks, n_heads, qp, 1), F32),
            pltpu.VMEM((n_blocks, n_heads, qp, 1), F32),
            pltpu.VMEM((n_blocks, n_heads, qp, hd), F32),
        ],
    )
    return pl.pallas_call(
        kern,
        grid_spec=grid_spec,
        out_shape=jax.ShapeDtypeStruct((n_seq, n_heads, qp, hd), F32),
        compiler_params=_cparams(("parallel", "arbitrary")),
        name="moba_sample",
    )(pt, q_hm, k_hm, v_hm, ck, ck, cv, cv)


def _row_tile(t, cap):
    tm = min(t, cap)
    assert t % tm == 0
    return tm


def kernel(x_prompt, x_sample, cache_a_k0, cache_a_v0, cache_a_k1, cache_a_v1, cache_a_k2, cache_a_v2, cache_b_k, cache_b_v, state_ffn_conv, page_table, norm_mix, norm_ffn, norm_kv, norm_final, w_qkv_a, w_o_a, w_kv_b, w_q_b, w_o_b, w_ffn_gate, w_ffn_up, ffn_conv_w, ffn_conv_b, w_ffn_down):
    n_p, seq, d = x_prompt.shape
    n_s, dec_seq, _ = x_sample.shape
    n_groups = len(DIL_GROUPS)
    width = w_o_a.shape[1]
    n_heads = width // HEAD_DIM
    d_ff = w_ffn_gate.shape[2]
    past_len = page_table.shape[1] * PAGE_SIZE
    assert w_qkv_a.shape[0] == 1 and w_q_b.shape[0] == 1, "one self-decoder and one cross-decoder layer"
    assert width == 1024 and seq % DIL_CHUNK_ROWS == 0 and seq % (SUBLANES * MOBA_BLOCK) == 0
    assert (n_s * dec_seq) % LANES == 0 and n_s % NEW_ROWS_PER_LANE_TILE == 0
    assert NEW_ROWS_PER_LANE_TILE * dec_seq == LANES and d_ff % FFN_CHUNK == 0
    assert all(dil & (dil - 1) == 0 and window // dil == BAND_BLOCK and BAND_BLOCK * dil <= DIL_CHUNK_ROWS
               for window, dil in DIL_GROUPS)

    wqkv = w_qkv_a[0].astype(BF16)
    wqkv_t = w_qkv_a[0].T.astype(BF16)
    wo_a = w_o_a[0].astype(BF16)
    wkv = w_kv_b.astype(BF16)
    wkv_t = w_kv_b.T.astype(BF16)
    wq_b = w_q_b[0].astype(BF16)
    wo_b = w_o_b[0].astype(BF16)
    wg = w_ffn_gate.astype(BF16)
    wu = w_ffn_up.astype(BF16)
    wd = w_ffn_down.astype(BF16)

    t_p = n_p * seq
    xp = x_prompt.reshape(t_p, d)
    tab_p, tab_p_fm = _rope_tables(jnp.tile(jnp.arange(seq, dtype=jnp.int32), n_p))
    tm_mm = _row_tile(t_p, 1024)
    tm_p = _row_tile(seq, 512)

    qkv = _norm_matmul(xp, norm_mix[0], wqkv, tab_p, rope_period=3, rope_count=2, tm=tm_mm)
    outs, lses, pa = [], [], []
    for g, (window, _) in enumerate(DIL_GROUPS):
        o, lse = _dilated_attention_prompt(qkv, g, n_p, seq, width, n_groups)
        outs.append(o)
        lses.append(lse)
        keep = min(window, seq)
        k_t, v_t = _proj_feature_major(xp, norm_mix[0], wqkv_t, g * 3 + 1, tab_p_fm, n_p, seq, keep,
                                       tr=min(keep, 512))
        pa.append(jnp.transpose(k_t[None], (0, 1, 4, 2, 3)))
        pa.append(jnp.transpose(v_t[None], (0, 1, 4, 2, 3)))
    h = _merge_oproj(xp, wo_a, outs, lses, tm=tm_p)
    h, tail0 = _conv_ffn(h, None, norm_ffn[0], wg[0], wu[0], wd[0], ffn_conv_w[0], ffn_conv_b[0], None,
                         tm=tm_p, conv_shift=1, seq_len=seq)
    k_tok = _norm_matmul(h, norm_kv, wkv, tab_p, rope_period=1, rope_count=1, tm=tm_mm, n_out=width)
    kb_t, vb_t = _proj_feature_major(h, norm_kv, wkv_t, 0, tab_p_fm, n_p, seq, seq, tr=512)
    kmean = _block_mean(k_tok).reshape(n_p, seq // MOBA_BLOCK, width)
    qb = _norm_matmul(h, norm_mix[1], wq_b, tab_p, rope_period=1, rope_count=1, tm=tm_mm)
    o = _moba_prompt(qb, k_tok, vb_t, kmean, n_p, seq, width)
    h = _merge_oproj(h, wo_b, [o], [], tm=tm_p)
    h, tail1, y_p = _conv_ffn(h, None, norm_ffn[1], wg[1], wu[1], wd[1], ffn_conv_w[1], ffn_conv_b[1], norm_final,
                              tm=tm_p, conv_shift=1, seq_len=seq)
    y_p = y_p.reshape(n_p, seq, d)
    pb_k = jnp.transpose(kb_t, (0, 3, 1, 2))
    pb_v = jnp.transpose(vb_t, (0, 3, 1, 2))
    tiles_per_seq = seq // tm_p

    def prompt_conv_state(tail):
        last = tail.reshape(n_p, tiles_per_seq, SUBLANES, d_ff)[:, -1]
        return last[:, SUBLANES - (CONV_WIDTH - 1):]

    p_conv = jnp.stack([prompt_conv_state(tail0), prompt_conv_state(tail1)], axis=0)

    t_s = n_s * dec_seq
    xs = jnp.transpose(x_sample, (1, 0, 2)).reshape(t_s, d)
    pos_s = past_len + jnp.repeat(jnp.arange(dec_seq, dtype=jnp.int32), n_s)
    tab_s, _ = _rope_tables(pos_s)

    def to_head_major(a):
        c = a.shape[-1] // width
        a = jnp.transpose(a.reshape(dec_seq, n_s, c, n_heads, HEAD_DIM), (2, 1, 3, 0, 4))
        return jnp.pad(a, ((0, 0), (0, 0), (0, 0), (0, SAMPLE_Q_PAD - dec_seq), (0, 0)))

    def to_time_major(a):
        return jnp.transpose(a[:, :, :dec_seq], (2, 0, 1, 3)).reshape(t_s, width)

    def conv_prev(state):
        return jnp.transpose(state, (1, 0, 2)).reshape((CONV_WIDTH - 1) * n_s, d_ff)

    def conv_next(tail):
        return jnp.transpose(tail.reshape(CONV_WIDTH - 1, n_s, d_ff), (1, 0, 2))

    qkv_s = _norm_matmul(xs, norm_mix[0], wqkv, tab_s, rope_period=3, rope_count=2, tm=t_s)
    qkv_hm = to_head_major(qkv_s)
    qkv_t3 = jnp.transpose(qkv_s.reshape(dec_seq, n_s, n_groups * 3 * n_heads, HEAD_DIM), (2, 3, 1, 0))
    qkv_t3 = qkv_t3.reshape(n_groups * 3 * n_heads, HEAD_DIM, t_s)
    caches = ((cache_a_k0, cache_a_v0), (cache_a_k1, cache_a_v1), (cache_a_k2, cache_a_v2))
    outs, lses, sa = [], [], []
    for g in range(n_groups):
        o, lse, ko, vo = _sample_window_attention(qkv_hm, qkv_t3, caches[g][0], caches[g][1], g, dec_seq)
        outs.append(to_time_major(o))
        lses.append(to_time_major(lse))
        sa += [ko, vo]
    hs = _merge_oproj(xs, wo_a, outs, lses, tm=t_s)
    hs, s_tail0 = _conv_ffn(hs, conv_prev(state_ffn_conv[0]), norm_ffn[0], wg[0], wu[0], wd[0], ffn_conv_w[0],
                            ffn_conv_b[0], None, tm=t_s, conv_shift=n_s, seq_len=dec_seq)
    kv_s = _norm_matmul(hs, norm_kv, wkv, tab_s, rope_period=2, rope_count=1, tm=t_s)
    qb_s = _norm_matmul(hs, norm_mix[1], wq_b, tab_s, rope_period=1, rope_count=1, tm=t_s)
    kv_hm = to_head_major(kv_s)
    o = _moba_sample(to_head_major(qb_s)[0], kv_hm[0], kv_hm[1], cache_b_k, cache_b_v, page_table, dec_seq)
    hs = _merge_oproj(hs, wo_b, [to_time_major(o)], [], tm=t_s)
    hs, s_tail1, y_s = _conv_ffn(hs, conv_prev(state_ffn_conv[1]), norm_ffn[1], wg[1], wu[1], wd[1], ffn_conv_w[1],
                                 ffn_conv_b[1], norm_final, tm=t_s, conv_shift=n_s, seq_len=dec_seq)
    y_s = jnp.transpose(y_s.reshape(dec_seq, n_s, d), (1, 0, 2))
    kv_s5 = jnp.transpose(kv_s.reshape(dec_seq, n_s, 2, n_heads, HEAD_DIM), (2, 1, 0, 3, 4))
    sb_k, sb_v = kv_s5[0], kv_s5[1]
    s_conv = jnp.stack([conv_next(s_tail0), conv_next(s_tail1)], axis=0)

    return (y_p, y_s, *pa, *sa, pb_k, pb_v, sb_k, sb_v, p_conv, s_conv)
```

```python
import functools

import jax
import jax.numpy as jnp
from jax import lax
from jax.experimental import pallas as pl
from jax.experimental.pallas import tpu as pltpu

F32 = jnp.float32
BF16 = jnp.bfloat16

HEAD_DIM = 64
DIL_GROUPS = ((128, 1), (512, 4), (2048, 16))
BAND_BLOCK = 128
MOBA_BLOCK = 256
MOBA_TOP_K = 3
PAGE_SIZE = 128
CONV_WIDTH = 3
ROPE_THETA = 10000.0
NORM_EPS = 1e-6
ATTN_SCALE = HEAD_DIM ** -0.5

LANES = 128
SUBLANES = 8
VMEM_LIMIT_BYTES = 56 * 1024 * 1024

HEADS_PER_VREG = LANES // HEAD_DIM
SAMPLE_Q_PAD = SUBLANES
NEG_BIG = -1e30
MASK_BIAS = -2e30


def _cparams(semantics):
    return pltpu.CompilerParams(dimension_semantics=semantics, vmem_limit_bytes=VMEM_LIMIT_BYTES)


def _nt_dot(a, b):
    return lax.dot_general(a, b, (((1,), (1,)), ((), ())), preferred_element_type=F32)


def _bdot(a, b, ca, cb):
    return lax.dot_general(a, b, (((ca,), (cb,)), ((0,), (0,))), preferred_element_type=F32)


def _split_bf16(x):
    hi = x.astype(BF16)
    lo = (x - hi.astype(F32)).astype(BF16)
    return hi, lo


def _dot_precise(a, b):
    a_hi, a_lo = _split_bf16(a)
    b_hi, b_lo = _split_bf16(b)
    dot = functools.partial(jnp.dot, preferred_element_type=F32)
    return dot(a_hi, b_hi) + (dot(a_hi, b_lo) + dot(a_lo, b_hi))


def _rmsnorm_val(x, gain):
    r = lax.rsqrt(jnp.mean(x * x, axis=-1, keepdims=True) + NORM_EPS)
    return (x * r) * gain


def _rope_tables(pos):
    half = HEAD_DIM // 2
    inv_freq = ROPE_THETA ** (-jnp.arange(half, dtype=F32) / half)
    ang = pos.astype(F32)[:, None] * inv_freq[None, :]
    cos, sin = jnp.cos(ang), jnp.sin(ang)
    zero = jnp.zeros_like(sin)
    reps = LANES // HEAD_DIM
    cos_t = jnp.tile(jnp.concatenate([cos, cos], axis=1), (1, reps))
    sin_lo = jnp.tile(jnp.concatenate([-sin, zero], axis=1), (1, reps))
    sin_hi = jnp.tile(jnp.concatenate([zero, sin], axis=1), (1, reps))
    return (cos_t, sin_lo, sin_hi), (cos.T, sin.T)


NORM_MM_CHUNK = 256


def _norm_mm_kernel(x_ref, g_ref, w_ref, cos_ref, slo_ref, shi_ref, o_ref, xn_ref, *, rope_period, rope_count):
    j = pl.program_id(1)

    @pl.when(j == 0)
    def _():
        xn_ref[...] = _rmsnorm_val(x_ref[...], g_ref[...]).astype(BF16)

    tn = o_ref.shape[1]
    half = HEAD_DIM // 2

    def chunks():
        for c in range(tn // NORM_MM_CHUNK):
            cols = slice(c * NORM_MM_CHUNK, (c + 1) * NORM_MM_CHUNK)
            yield c, jnp.dot(xn_ref[...], w_ref[:, cols], preferred_element_type=F32)

    def store_rope():
        cos, slo, shi = cos_ref[...], slo_ref[...], shi_ref[...]
        for c, y in chunks():
            for cc in range(NORM_MM_CHUNK // LANES):
                yc = y[:, cc * LANES:(cc + 1) * LANES]
                lo = c * NORM_MM_CHUNK + cc * LANES
                o_ref[:, lo:lo + LANES] = (
                    yc * cos + pltpu.roll(yc, LANES - half, 1) * slo + pltpu.roll(yc, half, 1) * shi)

    def store_plain():
        for c, y in chunks():
            o_ref[:, c * NORM_MM_CHUNK:(c + 1) * NORM_MM_CHUNK] = y

    if rope_count == rope_period:
        store_rope()
    else:
        is_rope = (j % rope_period) < rope_count
        pl.when(is_rope)(store_rope)
        pl.when(jnp.logical_not(is_rope))(store_plain)


def _norm_matmul(x, gain, w_bf16, tables, *, rope_period, rope_count, tm, n_out=None, tn=1024):
    t, d = x.shape
    n = w_bf16.shape[1] if n_out is None else n_out
    kern = functools.partial(_norm_mm_kernel, rope_period=rope_period, rope_count=rope_count)
    tab_spec = pl.BlockSpec((tm, LANES), lambda i, j: (i, 0))
    return pl.pallas_call(
        kern,
        grid=(t // tm, n // tn),
        in_specs=[
            pl.BlockSpec((tm, d), lambda i, j: (i, 0)),
            pl.BlockSpec((1, d), lambda i, j: (0, 0)),
            pl.BlockSpec((d, tn), lambda i, j: (0, j)),
            tab_spec, tab_spec, tab_spec,
        ],
        out_specs=pl.BlockSpec((tm, tn), lambda i, j: (i, j)),
        out_shape=jax.ShapeDtypeStruct((t, n), F32),
        scratch_shapes=[pltpu.VMEM((tm, d), BF16)],
        compiler_params=_cparams(("parallel", "arbitrary")),
        name="norm_matmul",
    )(x, gain.reshape(1, d), w_bf16, *tables)


def _proj_fm_kernel(x_ref, g_ref, wt_ref, cos_ref, sin_ref, k_ref, v_ref, xn_ref):
    j = pl.program_id(1)
    half = HEAD_DIM // 2

    @pl.when(j == 0)
    def _():
        xn_ref[...] = _rmsnorm_val(x_ref[...], g_ref[...]).astype(BF16)

    def project():
        y = _nt_dot(wt_ref[...], xn_ref[...])
        return y.reshape(y.shape[0] // HEAD_DIM, HEAD_DIM, y.shape[1])

    @pl.when(j == 0)
    def _():
        y = project()
        cos, sin = cos_ref[...], sin_ref[...]
        x1, x2 = y[:, :half, :], y[:, half:, :]
        k_ref[0] = jnp.concatenate([x1 * cos - x2 * sin, x2 * cos + x1 * sin], axis=1)

    @pl.when(j == 1)
    def _():
        v_ref[0] = project()


def _proj_feature_major(x, gain, wt_bf16, w_block, tables_fm, n_seq, seq_len, keep, *, tr):
    d = x.shape[1]
    width = 1024
    n_heads = width // HEAD_DIM
    tiles = keep // tr
    seq_tiles = seq_len // tr
    first = seq_tiles - tiles

    def rows(i):
        return (i // tiles) * seq_tiles + first + (i % tiles)

    out_spec = pl.BlockSpec((1, n_heads, HEAD_DIM, tr), lambda i, j: (i // tiles, 0, 0, i % tiles))
    out_sds = jax.ShapeDtypeStruct((n_seq, n_heads, HEAD_DIM, keep), F32)
    tab_spec = pl.BlockSpec((HEAD_DIM // 2, tr), lambda i, j: (0, rows(i)))
    return pl.pallas_call(
        _proj_fm_kernel,
        grid=(n_seq * tiles, 2),
        in_specs=[
            pl.BlockSpec((tr, d), lambda i, j: (rows(i), 0)),
            pl.BlockSpec((1, d), lambda i, j: (0, 0)),
            pl.BlockSpec((width, d), lambda i, j: (w_block + j, 0)),
            tab_spec, tab_spec,
        ],
        out_specs=[out_spec, out_spec],
        out_shape=[out_sds, out_sds],
        scratch_shapes=[pltpu.VMEM((tr, d), BF16)],
        compiler_params=_cparams(("parallel", "arbitrary")),
        name="proj_feature_major",
    )(x, gain.reshape(1, d), wt_bf16, *tables_fm)


DIL_CHUNK_ROWS = 2048
DIL_HEAD_LANES = LANES


def _dil_attn_kernel(q_ref, k_ref, v_ref, kp_ref, vp_ref, o_ref, l_ref, *, dil, n_back):
    i = pl.program_id(1)
    blk = BAND_BLOCK
    span = blk * dil
    n_units = q_ref.shape[0] // blk
    pair_w = HEADS_PER_VREG * blk
    ki = lax.broadcasted_iota(jnp.int32, (2 * blk, pair_w), 0)
    qi = (lax.broadcasted_iota(jnp.int32, (2 * blk, pair_w), 1) & (blk - 1)) + blk
    dist = qi - ki
    in_band = (dist >= 0) & (dist <= n_back)
    is_cur = ki >= blk
    first_head = lax.broadcasted_iota(jnp.int32, (blk, LANES), 1) < HEAD_DIM

    def rows(ref, start):
        if dil == 1:
            return ref[pl.ds(pl.multiple_of(start, blk), blk), :]
        return ref[pl.ds(start, blk, stride=dil), :]

    def unit_body(u, carry):
        sb = u // dil
        r = u % dil
        start = sb * span + r
        prev_here = sb > 0
        prev_start = jnp.maximum(start - span, 0)
        q2 = rows(q_ref, start) * ATTN_SCALE
        q_bd = jnp.concatenate([jnp.where(first_head, q2, 0.0), jnp.where(first_head, 0.0, q2)], axis=0).astype(BF16)
        k_prev = jnp.where(prev_here, rows(k_ref, prev_start), rows(kp_ref, r))
        v_prev = jnp.where(prev_here, rows(v_ref, prev_start), rows(vp_ref, r))
        k2 = jnp.concatenate([k_prev, rows(k_ref, start)], axis=0).astype(BF16)
        v2 = jnp.concatenate([v_prev, rows(v_ref, start)], axis=0).astype(BF16)
        mask = in_band & (is_cur | (prev_here | (i > 0)))
        s = jnp.where(mask, _nt_dot(k2, q_bd), -jnp.inf)
        m = jnp.max(s, axis=0, keepdims=True)
        e = jnp.exp(s - m)
        den = jnp.sum(e, axis=0, keepdims=True)
        prob = (e * (1.0 / den)).astype(BF16)
        lse = m + jnp.log(den)
        outs, lses = [], []
        for h in range(HEADS_PER_VREG):
            outs.append(lax.dot_general(prob[:, h * blk:(h + 1) * blk], v2[:, h * HEAD_DIM:(h + 1) * HEAD_DIM],
                                        (((0,), (0,)), ((), ())), preferred_element_type=F32))
            lses.append(jnp.broadcast_to(lse[:, h * blk:(h + 1) * blk], (HEAD_DIM, blk)))
        o_val = jnp.concatenate(outs, axis=1)
        l_val = jnp.concatenate(lses, axis=0).T
        if dil == 1:
            o_ref[pl.ds(pl.multiple_of(start, blk), blk), :] = o_val
            l_ref[pl.ds(pl.multiple_of(start, blk), blk), :] = l_val
        else:
            o_ref[pl.ds(start, blk, stride=dil), :] = o_val
            l_ref[pl.ds(start, blk, stride=dil), :] = l_val
        return carry

    lax.fori_loop(0, n_units, unit_body, 0, unroll=4)


def _dilated_attention_prompt(qkv, group, n_seq, seq_len, width, n_groups):
    window, dil = DIL_GROUPS[group]
    rows, hl = DIL_CHUNK_ROWS, DIL_HEAD_LANES
    span = BAND_BLOCK * dil
    chunks = seq_len // rows
    per_w = width // hl
    base = group * 3

    def cur(which):
        return pl.BlockSpec((rows, hl), lambda b, i, c: (b * chunks + i, (base + which) * per_w + c))

    def prev(which):
        return pl.BlockSpec(
            (span, hl),
            lambda b, i, c: (jnp.maximum((b * chunks + i) * (rows // span) - 1, 0), (base + which) * per_w + c))

    out_spec = pl.BlockSpec((rows, hl), lambda b, i, c: (b * chunks + i, c))
    out_sds = jax.ShapeDtypeStruct((n_seq * seq_len, width), F32)
    kern = functools.partial(_dil_attn_kernel, dil=dil, n_back=window // dil)
    return pl.pallas_call(
        kern,
        grid=(n_seq, chunks, per_w),
        in_specs=[cur(0), cur(1), cur(2), prev(1), prev(2)],
        out_specs=[out_spec, out_spec],
        out_shape=[out_sds, out_sds],
        compiler_params=_cparams(("parallel", "parallel", "parallel")),
        name="dilated_attn_prompt",
    )(qkv, qkv, qkv, qkv, qkv)


def _merge_oproj_kernel(*refs, n_groups):
    x_ref, w_ref = refs[0], refs[1]
    o_refs = refs[2:2 + n_groups]
    l_refs = refs[2 + n_groups:2 + 2 * n_groups]
    out_ref = refs[-1]
    if n_groups == 1:
        o = o_refs[0][...]
    else:
        ls = [r[...] for r in l_refs]
        m = functools.reduce(jnp.maximum, ls)
        es = [jnp.exp(l - m) for l in ls]
        den = functools.reduce(lambda a, b: a + b, es)
        o = None
        for e, o_ref in zip(es, o_refs):
            term = (e / den) * o_ref[...]
            o = term if o is None else o + term
    out_ref[...] = x_ref[...] + jnp.dot(o.astype(BF16), w_ref[...], preferred_element_type=F32)


def _merge_oproj(x, w_bf16, outs, lses, *, tm):
    t, d = x.shape
    width = w_bf16.shape[0]
    n_groups = len(outs)
    row_spec = pl.BlockSpec((tm, width), lambda i: (i, 0))
    kern = functools.partial(_merge_oproj_kernel, n_groups=n_groups)
    return pl.pallas_call(
        kern,
        grid=(t // tm,),
        in_specs=[pl.BlockSpec((tm, d), lambda i: (i, 0)), pl.BlockSpec((width, d), lambda i: (0, 0))]
        + [row_spec] * (n_groups + len(lses)),
        out_specs=pl.BlockSpec((tm, d), lambda i: (i, 0)),
        out_shape=jax.ShapeDtypeStruct((t, d), F32),
        compiler_params=_cparams(("parallel",)),
        name="merge_oproj",
    )(x, w_bf16, *outs, *lses)


FFN_CHUNK = 256


def _ffn_kernel(*refs, conv_shift, tiles_per_seq, final):
    it = iter(refs)
    h_ref, prev_ref, gn_ref, wg_ref, wu_ref, wd_ref, cw_ref, cb_ref = (next(it) for _ in range(8))
    gf_ref = next(it) if final else None
    out_ref, tail_ref = next(it), next(it)
    y_ref = next(it) if final else None
    xn_ref, acc_ref = next(it), next(it)

    i = pl.program_id(0)
    tm = h_ref.shape[0]
    n_chunks = wg_ref.shape[1] // FFN_CHUNK
    x = h_ref[...]
    xn_ref[...] = _rmsnorm_val(x, gn_ref[...]).astype(BF16)
    acc_ref[...] = jnp.zeros_like(acc_ref)
    if conv_shift == 1:
        xn_prev = _rmsnorm_val(prev_ref[...], gn_ref[...]).astype(BF16)
        has_prev = (i % tiles_per_seq) != 0
        row = lax.broadcasted_iota(jnp.int32, (tm, FFN_CHUNK), 0)

    def chunk_body(c, carry):
        off = pl.multiple_of(c * FFN_CHUNK, FFN_CHUNK)
        wg = wg_ref[:, pl.ds(off, FFN_CHUNK)]
        g = jnp.dot(xn_ref[...], wg, preferred_element_type=F32)
        u = jnp.dot(xn_ref[...], wu_ref[:, pl.ds(off, FFN_CHUNK)], preferred_element_type=F32)
        if conv_shift == 1:
            gp = jnp.dot(xn_prev, wg, preferred_element_type=F32)
            gp = jnp.where(has_prev, gp, 0.0)
            last, last2 = gp[SUBLANES - 1:SUBLANES, :], gp[SUBLANES - 2:SUBLANES - 1, :]
            g1 = jnp.where(row == 0, last, pltpu.roll(g, 1, 0))
            g2 = jnp.where(row == 0, last2, jnp.where(row == 1, last, pltpu.roll(g, 2, 0)))
            tail_ref[0, :, pl.ds(off, FFN_CHUNK)] = g[tm - SUBLANES:, :]
        else:
            ext = jnp.concatenate([prev_ref[:, pl.ds(off, FFN_CHUNK)], g], axis=0)
            g2 = ext[:tm, :]
            g1 = ext[conv_shift:conv_shift + tm, :]
            tail_ref[:, pl.ds(off, FFN_CHUNK)] = ext[tm:, :]
        cw = cw_ref[:, pl.ds(off, FFN_CHUNK)]
        conv = cb_ref[:, pl.ds(off, FFN_CHUNK)] + ((cw[0:1, :] * g2 + cw[1:2, :] * g1) + cw[2:3, :] * g)
        act = (conv * (1.0 / (1.0 + jnp.exp(-conv)))) * u
        acc_ref[...] += jnp.dot(act.astype(BF16), wd_ref[pl.ds(off, FFN_CHUNK), :], preferred_element_type=F32)
        return carry

    lax.fori_loop(0, n_chunks, chunk_body, 0, unroll=True)
    h_out = x + acc_ref[...]
    out_ref[...] = h_out
    if final:
        y_ref[...] = _rmsnorm_val(h_out, gf_ref[...])


def _conv_ffn(h, prev, gain, wg, wu, wd, conv_w, conv_b, final_gain, *, tm, conv_shift, seq_len):
    t, d = h.shape
    d_ff = wg.shape[1]
    final = final_gain is not None
    const2 = lambda i: (0, 0)
    in_specs = [pl.BlockSpec((tm, d), lambda i: (i, 0))]
    args = [h]
    if conv_shift == 1:
        rows8 = tm // SUBLANES
        in_specs.append(pl.BlockSpec((SUBLANES, d), lambda i: (jnp.maximum(i * rows8 - 1, 0), 0)))
        args.append(h)
        tail_shape = jax.ShapeDtypeStruct((t // tm, SUBLANES, d_ff), F32)
        tail_spec = pl.BlockSpec((1, SUBLANES, d_ff), lambda i: (i, 0, 0))
        tiles_per_seq = seq_len // tm
    else:
        assert t == tm
        n_prev = (CONV_WIDTH - 1) * conv_shift
        in_specs.append(pl.BlockSpec((n_prev, d_ff), const2))
        args.append(prev)
        tail_shape = jax.ShapeDtypeStruct((n_prev, d_ff), F32)
        tail_spec = pl.BlockSpec((n_prev, d_ff), const2)
        tiles_per_seq = 1
    resident = pl.Buffered(1)
    in_specs += [
        pl.BlockSpec((1, d), const2),
        pl.BlockSpec((d, d_ff), const2, pipeline_mode=resident),
        pl.BlockSpec((d, d_ff), const2, pipeline_mode=resident),
        pl.BlockSpec((d_ff, d), const2, pipeline_mode=resident),
        pl.BlockSpec((CONV_WIDTH, d_ff), const2), pl.BlockSpec((1, d_ff), const2),
    ]
    args += [gain.reshape(1, d), wg, wu, wd, conv_w, conv_b.reshape(1, d_ff)]
    out_specs = [pl.BlockSpec((tm, d), lambda i: (i, 0)), tail_spec]
    out_shape = [jax.ShapeDtypeStruct((t, d), F32), tail_shape]
    if final:
        in_specs.append(pl.BlockSpec((1, d), const2))
        args.append(final_gain.reshape(1, d))
        out_specs.append(pl.BlockSpec((tm, d), lambda i: (i, 0)))
        out_shape.append(jax.ShapeDtypeStruct((t, d), F32))
    kern = functools.partial(_ffn_kernel, conv_shift=conv_shift, tiles_per_seq=tiles_per_seq, final=final)
    return pl.pallas_call(
        kern,
        grid=(t // tm,),
        in_specs=in_specs,
        out_specs=out_specs,
        out_shape=out_shape,
        scratch_shapes=[pltpu.VMEM((tm, d), BF16), pltpu.VMEM((tm, d), F32)],
        compiler_params=_cparams(("parallel",)),
        name="conv_ffn",
    )(*args)


def _block_mean_kernel(k_ref, o_ref):
    for r in range(o_ref.shape[0]):
        o_ref[r:r + 1, :] = jnp.mean(k_ref[r * MOBA_BLOCK:(r + 1) * MOBA_BLOCK, :], axis=0, keepdims=True)


def _block_mean(k):
    t, width = k.shape
    rows = SUBLANES * MOBA_BLOCK
    return pl.pallas_call(
        _block_mean_kernel,
        grid=(t // rows,),
        in_specs=[pl.BlockSpec((rows, width), lambda i: (i, 0))],
        out_specs=pl.BlockSpec((SUBLANES, width), lambda i: (i, 0)),
        out_shape=jax.ShapeDtypeStruct((t // MOBA_BLOCK, width), F32),
        compiler_params=_cparams(("parallel",)),
        name="moba_block_mean",
    )(k)


def _top_k_mask(gate, n_valid, k_sel, axis):
    nb = gate.shape[axis]
    blk_id = lax.broadcasted_iota(jnp.int32, gate.shape, axis)
    sel = jnp.zeros(gate.shape, F32)
    g = gate
    for kk in range(k_sel):
        mx = jnp.max(g, axis=axis, keepdims=True)
        idx = jnp.min(jnp.where(g == mx, blk_id, nb), axis=axis, keepdims=True)
        hit = blk_id == idx
        counts = jnp.where(kk < n_valid, 1.0, 0.0)
        sel = jnp.maximum(sel, jnp.where(hit, counts, 0.0))
        g = jnp.where(hit, -jnp.inf, g)
    return sel


def _moba_prompt_kernel(qi_tab, kj_tab, q_ref, k_ref, vt_ref, km_ref, *rest,
                        n_pairs, k_sel, steps_per_seq, rider_steps, rider_chunks, rider_cfg):
    n_rider_in, n_rider_out = (7, 4) if rider_steps else (0, 0)
    rider_in = rest[:n_rider_in]
    o_ref = rest[n_rider_in]
    rider_out = rest[n_rider_in + 1:n_rider_in + 1 + n_rider_out]
    qbd_ref, sel_ref, m_ref, l_ref, a_ref, acc_ref, s_ref, p_ref = rest[n_rider_in + 1 + n_rider_out:]
    step = pl.program_id(0)

    if rider_steps:
        @pl.when(step < rider_steps)
        def _():
            _sample_window_body(step // rider_chunks, *rider_in, *rider_out, **rider_cfg)

    t = step % steps_per_seq
    qi = qi_tab[t]
    kj = kj_tab[t]
    blk = MOBA_BLOCK
    nb = km_ref.shape[1]
    is_own = kj == qi

    pair_w = HEADS_PER_VREG * blk

    @pl.when(kj == 0)
    def _():
        m_ref[...] = jnp.full(m_ref.shape, NEG_BIG, F32)
        l_ref[...] = jnp.zeros_like(l_ref)
        acc_ref[...] = jnp.zeros_like(acc_ref)
        blk_id = lax.broadcasted_iota(jnp.int32, (nb, pair_w), 0)
        feat = lax.broadcasted_iota(jnp.int32, (LANES, blk), 0)
        for p in range(n_pairs):
            q_t = q_ref[:, p * LANES:(p + 1) * LANES].T
            q_bd = jnp.concatenate([jnp.where(feat < HEAD_DIM, q_t, 0.0),
                                    jnp.where(feat < HEAD_DIM, 0.0, q_t)], axis=1)
            qbd_ref[p] = (q_bd * ATTN_SCALE).astype(BF16)
            gate = _dot_precise(km_ref[0, :, p * LANES:(p + 1) * LANES], q_bd)
            gate = jnp.where(blk_id < qi, gate, -jnp.inf)
            sel_ref[p] = _top_k_mask(gate, qi, k_sel, 0)

    for p in range(n_pairs):
        s_ref[p] = jnp.dot(k_ref[:, p * LANES:(p + 1) * LANES].astype(BF16), qbd_ref[p],
                           preferred_element_type=F32)

    @pl.when(is_own)
    def _():
        key = lax.broadcasted_iota(jnp.int32, (blk, pair_w), 0)
        qry = lax.broadcasted_iota(jnp.int32, (blk, pair_w), 1) & (blk - 1)
        bias = jnp.where(key > qry, MASK_BIAS, 0.0)
        for p in range(n_pairs):
            s_ref[p] = s_ref[p] + bias

    takes_part = jnp.where(is_own, 1.0, sel_ref[:, pl.ds(kj, 1), :]) > 0.5
    s = s_ref[...]
    m_old = m_ref[...]
    m_new = jnp.where(takes_part, jnp.maximum(m_old, jnp.max(s, axis=1, keepdims=True)), m_old)
    alpha = jnp.exp(m_old - m_new)
    prob = jnp.exp(s - jnp.where(takes_part, m_new, -NEG_BIG))
    l_ref[...] = alpha * l_ref[...] + jnp.sum(prob, axis=1, keepdims=True)
    m_ref[...] = m_new
    a_ref[...] = alpha
    p_ref[...] = prob.astype(BF16)
    for h in range(n_pairs * HEADS_PER_VREG):
        p, c = divmod(h, HEADS_PER_VREG)
        qs = slice(c * blk, (c + 1) * blk)
        acc_ref[h] = a_ref[p, :, qs] * acc_ref[h] + jnp.dot(vt_ref[0, h].astype(BF16), p_ref[p, :, qs],
                                                             preferred_element_type=F32)

    @pl.when(is_own)
    def _():
        for p in range(n_pairs):
            o_t = jnp.concatenate(
                [acc_ref[p * HEADS_PER_VREG + c] / l_ref[p, :, c * blk:(c + 1) * blk] for c in range(HEADS_PER_VREG)],
                axis=0)
            o_ref[:, p * LANES:(p + 1) * LANES] = o_t.T


def _moba_prompt(q, k, v_t, kmean, n_seq, seq_len, width, rider=None):
    nb = seq_len // MOBA_BLOCK
    n_heads = width // HEAD_DIM
    pairs = [(a, b) for a in range(nb) for b in range(a + 1)]
    steps = len(pairs)
    qi_tab = jnp.asarray([p[0] for p in pairs], jnp.int32)
    kj_tab = jnp.asarray([p[1] for p in pairs], jnp.int32)
    n_pairs = width // LANES
    pair_w = HEADS_PER_VREG * MOBA_BLOCK
    plan = None
    if rider is not None:
        def wrap(f):
            def index_map(s, qt, kt):
                r = jnp.minimum(s, plan_steps - 1)
                return f(r // plan_chunks, r % plan_chunks)
            return index_map

        plan = _sample_window_plan(*rider, wrap)
        plan_chunks = plan["chunks"]
        plan_steps = plan["n_seq"] * plan_chunks
        assert plan_steps <= n_seq * steps
    kern = functools.partial(
        _moba_prompt_kernel, n_pairs=n_pairs, k_sel=min(MOBA_TOP_K, nb), steps_per_seq=steps,
        rider_steps=plan_steps if plan else 0, rider_chunks=plan_chunks if plan else 1,
        rider_cfg=plan["cfg"] if plan else None)
    o_spec = pl.BlockSpec((MOBA_BLOCK, width), lambda s, qt, kt: ((s // steps) * nb + qt[s % steps], 0))
    o_sds = jax.ShapeDtypeStruct((n_seq * seq_len, width), F32)
    grid_spec = pltpu.PrefetchScalarGridSpec(
        num_scalar_prefetch=2,
        grid=(n_seq * steps,),
        in_specs=[
            pl.BlockSpec((MOBA_BLOCK, width), lambda s, qt, kt: ((s // steps) * nb + qt[s % steps], 0)),
            pl.BlockSpec((MOBA_BLOCK, width), lambda s, qt, kt: ((s // steps) * nb + kt[s % steps], 0)),
            pl.BlockSpec((1, n_heads, HEAD_DIM, MOBA_BLOCK), lambda s, qt, kt: (s // steps, 0, 0, kt[s % steps])),
            pl.BlockSpec((1, nb, width), lambda s, qt, kt: (s // steps, 0, 0)),
        ] + (plan["in_specs"] if plan else []),
        out_specs=[o_spec] + (plan["out_specs"] if plan else []),
        scratch_shapes=[
            pltpu.VMEM((n_pairs, LANES, pair_w), BF16),
            pltpu.VMEM((n_pairs, nb, pair_w), F32),
            pltpu.VMEM((n_pairs, 1, pair_w), F32),
            pltpu.VMEM((n_pairs, 1, pair_w), F32),
            pltpu.VMEM((n_pairs, 1, pair_w), F32),
            pltpu.VMEM((n_heads, HEAD_DIM, MOBA_BLOCK), F32),
            pltpu.VMEM((n_pairs, MOBA_BLOCK, pair_w), F32),
            pltpu.VMEM((n_pairs, MOBA_BLOCK, pair_w), BF16),
        ],
    )
    return pl.pallas_call(
        kern,
        grid_spec=grid_spec,
        out_shape=[o_sds] + (plan["out_shape"] if plan else []),
        compiler_params=_cparams(("arbitrary",)),
        name="moba_prompt",
    )(qi_tab, kj_tab, q, k, v_t, kmean, *(plan["args"] if plan else []))


SAMPLE_STEP_BYTES = 2 * 1024 * 1024
RIDER_HEAD_CHUNK = 2
NEW_ROWS_PER_LANE_TILE = 32


def _sample_window_kernel(*refs, **cfg):
    _sample_window_body(pl.program_id(0), *refs, **cfg)


def _sample_window_body(n, q_ref, kn_ref, vn_ref, knt_ref, vnt_ref, kc_ref, vc_ref,
                        ko_ref, vo_ref, o_ref, l_ref, *, window, dil, n_new):
    hc, hd, n_buf = kc_ref.shape[2:]
    qp = q_ref.shape[3]
    shift = (LANES - n_new - (n % NEW_ROWS_PER_LANE_TILE) * n_new) % LANES
    lane = lax.broadcasted_iota(jnp.int32, (hc * hd, LANES), 1)
    is_new_lane = lane >= LANES - n_new

    for c_ref, nt_ref, out_ref in ((kc_ref, knt_ref, ko_ref), (vc_ref, vnt_ref, vo_ref)):
        rolled = pltpu.roll(c_ref[0, 0].reshape(hc * hd, n_buf), n_buf - n_new, 1)
        new_t = pltpu.roll(nt_ref[...].reshape(hc * hd, LANES), shift, 1)
        if n_buf > LANES:
            out_ref[0, 0, :, :, :n_buf - LANES] = rolled[:, :n_buf - LANES].reshape(hc, hd, n_buf - LANES)
        last = jnp.where(is_new_lane, new_t, rolled[:, n_buf - LANES:])
        out_ref[0, 0, :, :, n_buf - LANES:] = last.reshape(hc, hd, LANES)

    q_idx = lax.broadcasted_iota(jnp.int32, (qp, n_buf), 0)
    r_idx = lax.broadcasted_iota(jnp.int32, (qp, n_buf), 1)
    dist_c = n_buf + q_idx - r_idx
    dil_mask = dil - 1
    valid_c = ((dist_c & dil_mask) == 0) & (dist_c <= window)
    qn_idx = lax.broadcasted_iota(jnp.int32, (qp, qp), 0)
    jn_idx = lax.broadcasted_iota(jnp.int32, (qp, qp), 1)
    dist_n = qn_idx - jn_idx
    valid_n = (dist_n >= 0) & ((dist_n & dil_mask) == 0) & (dist_n <= window) & ((jn_idx < n_new) | (jn_idx == qn_idx))

    q = q_ref[0, 0].astype(BF16)
    s_c = _bdot(q, kc_ref[0, 0].astype(BF16), 2, 1) * ATTN_SCALE
    s_c = jnp.where(valid_c, s_c, -jnp.inf)
    s_n = _bdot(q, kn_ref[0, 0].astype(BF16), 2, 2) * ATTN_SCALE
    s_n = jnp.where(valid_n, s_n, -jnp.inf)
    m = jnp.maximum(jnp.max(s_c, axis=-1, keepdims=True), jnp.max(s_n, axis=-1, keepdims=True))
    e_c = jnp.exp(s_c - m)
    e_n = jnp.exp(s_n - m)
    den = jnp.sum(e_c, axis=-1, keepdims=True) + jnp.sum(e_n, axis=-1, keepdims=True)
    o = _bdot(e_c.astype(BF16), vc_ref[0, 0].astype(BF16), 2, 2)
    o = o + _bdot(e_n.astype(BF16), vn_ref[0, 0].astype(BF16), 2, 1)
    o_ref[0] = o / den
    l_ref[0] = jnp.broadcast_to(m + jnp.log(den), o.shape)


def _sample_window_attention(qkv_hm, qkv_t3, cache_k, cache_v, group, n_new):
    _, n_seq, n_buf, n_heads, hd = cache_k.shape
    hc = max(1, min(n_heads, SAMPLE_STEP_BYTES // (hd * n_buf * 4)))
    plan = _sample_window_plan(qkv_hm, qkv_t3, cache_k, cache_v, group, n_new, hc, lambda f: f)
    kern = functools.partial(_sample_window_kernel, **plan["cfg"])
    results = pl.pallas_call(
        kern,
        grid=(n_seq, plan["chunks"]),
        in_specs=plan["in_specs"],
        out_specs=plan["out_specs"],
        out_shape=plan["out_shape"],
        compiler_params=_cparams(("parallel", "parallel")),
        name="sample_window_attn",
    )(*plan["args"])
    return _sample_window_results(results)


def _sample_window_results(results):
    ko, vo, o, lse = results
    back = (0, 1, 4, 2, 3)
    return o, lse, jnp.transpose(ko, back), jnp.transpose(vo, back)


def _sample_window_plan(qkv_hm, qkv_t3, cache_k, cache_v, group, n_new, hc, wrap):
    window, dil = DIL_GROUPS[group]
    _, n_seq, n_buf, n_heads, hd = cache_k.shape
    qp = qkv_hm.shape[3]
    chunks = n_heads // hc
    ck = jnp.transpose(cache_k, (0, 1, 3, 4, 2))
    cv = jnp.transpose(cache_v, (0, 1, 3, 4, 2))
    base = group * 3

    def hm_spec(which):
        return pl.BlockSpec((1, 1, hc, qp, hd), wrap(lambda n, c: (base + which, n, c, 0, 0)))

    def t_spec(which):
        return pl.BlockSpec((hc, hd, LANES),
                            wrap(lambda n, c: ((base + which) * chunks + c, 0, n // NEW_ROWS_PER_LANE_TILE)))

    cache_spec = pl.BlockSpec((1, 1, hc, hd, n_buf), wrap(lambda n, c: (0, n, c, 0, 0)))
    out_spec = pl.BlockSpec((1, hc, qp, hd), wrap(lambda n, c: (n, c, 0, 0)))
    out_sds = jax.ShapeDtypeStruct((n_seq, n_heads, qp, hd), F32)
    return dict(
        args=[qkv_hm, qkv_hm, qkv_hm, qkv_t3, qkv_t3, ck, cv],
        in_specs=[hm_spec(0), hm_spec(1), hm_spec(2), t_spec(1), t_spec(2), cache_spec, cache_spec],
        out_specs=[cache_spec, cache_spec, out_spec, out_spec],
        out_shape=[jax.ShapeDtypeStruct(ck.shape, F32), jax.ShapeDtypeStruct(cv.shape, F32), out_sds, out_sds],
        cfg=dict(window=window, dil=dil, n_new=n_new), chunks=chunks, n_seq=n_seq)


def _moba_sample_kernel(pt_ref, q_ref, kn_ref, vn_ref, *refs, n_new, k_sel, n_pages, pages_per_block):
    k_refs, v_refs, o_ref = refs[:n_pages], refs[n_pages:2 * n_pages], refs[2 * n_pages]
    n_heads, qp, _ = q_ref.shape[1:]
    n_blocks = n_pages // pages_per_block
    q = q_ref[0].astype(BF16)
    raw = [_bdot(q, k_ref[0].astype(BF16), 2, 1) for k_ref in k_refs]
    blk_lane = lax.broadcasted_iota(jnp.int32, (n_heads, qp, LANES), 2)
    gate = jnp.full((n_heads, qp, LANES), -jnp.inf, F32)
    for bb in range(n_blocks):
        pages = raw[bb * pages_per_block:(bb + 1) * pages_per_block]
        total = functools.reduce(lambda a, b: a + b, [jnp.sum(r, axis=-1, keepdims=True) for r in pages])
        gate = jnp.where(blk_lane == bb, total / MOBA_BLOCK, gate)
    picked = _top_k_mask(gate, n_blocks, k_sel, 2)
    takes_part = [jnp.max(jnp.where(blk_lane == bb, picked, 0.0), axis=-1, keepdims=True) > 0.5
                  for bb in range(n_blocks)]

    qn_idx = lax.broadcasted_iota(jnp.int32, (qp, qp), 0)
    jn_idx = lax.broadcasted_iota(jnp.int32, (qp, qp), 1)
    valid_own = (jn_idx <= qn_idx) & ((jn_idx < n_new) | (jn_idx == qn_idx))
    s_own = jnp.where(valid_own, _bdot(q, kn_ref[0].astype(BF16), 2, 2) * ATTN_SCALE, -jnp.inf)
    m_all = jnp.max(s_own, axis=-1, keepdims=True)
    for j, r in enumerate(raw):
        page_max = jnp.max(r, axis=-1, keepdims=True) * ATTN_SCALE
        m_all = jnp.maximum(m_all, jnp.where(takes_part[j // pages_per_block], page_max, NEG_BIG))
    e_own = jnp.exp(s_own - m_all)
    den = jnp.sum(e_own, axis=-1, keepdims=True)
    num = _bdot(e_own.astype(BF16), vn_ref[0].astype(BF16), 2, 1)
    for j, (r, v_ref) in enumerate(zip(raw, v_refs)):
        e = jnp.where(takes_part[j // pages_per_block], jnp.exp(r * ATTN_SCALE - m_all), 0.0)
        den = den + jnp.sum(e, axis=-1, keepdims=True)
        num = num + _bdot(e.astype(BF16), v_ref[0].astype(BF16), 2, 2)
    o_ref[0] = num / den


def _moba_sample(q_hm, k_hm, v_hm, cache_k, cache_v, page_table, n_new):
    n_seq, n_pages = page_table.shape
    _, n_heads, qp, hd = q_hm.shape
    pages_per_block = MOBA_BLOCK // PAGE_SIZE
    assert pages_per_block == 2 and n_pages % pages_per_block == 0 and n_new <= SAMPLE_Q_PAD
    n_blocks = n_pages // pages_per_block
    ck = jnp.transpose(cache_k, (0, 2, 3, 1))
    cv = jnp.transpose(cache_v, (0, 2, 3, 1))
    pt = page_table.reshape(-1).astype(jnp.int32)

    def page_spec(j):
        return pl.BlockSpec((1, n_heads, hd, PAGE_SIZE), lambda n, pt_ref: (pt_ref[n * n_pages + j], 0, 0, 0))

    seq_spec = pl.BlockSpec((1, n_heads, qp, hd), lambda n, pt_ref: (n, 0, 0, 0))
    page_specs = [page_spec(j) for j in range(n_pages)]
    kern = functools.partial(_moba_sample_kernel, n_new=n_new, k_sel=min(MOBA_TOP_K, n_blocks + 1),
                             n_pages=n_pages, pages_per_block=pages_per_block)
    grid_spec = pltpu.PrefetchScalarGridSpec(
        num_scalar_prefetch=1,
        grid=(n_seq,),
        in_specs=[seq_spec, seq_spec, seq_spec] + page_specs + page_specs,
        out_specs=seq_spec,
    )
    return pl.pallas_call(
        kern,
        grid_spec=grid_spec,
        out_shape=jax.ShapeDtypeStruct((n_seq, n_heads, qp, hd), F32),
        compiler_params=_cparams(("parallel",)),
        name="moba_sample",
    )(pt, q_hm, k_hm, v_hm, *([ck] * n_pages), *([cv] * n_pages))


def _row_tile(t, cap):
    tm = min(t, cap)
    assert t % tm == 0
    return tm


def kernel(x_prompt, x_sample, cache_a_k0, cache_a_v0, cache_a_k1, cache_a_v1, cache_a_k2, cache_a_v2, cache_b_k, cache_b_v, state_ffn_conv, page_table, norm_mix, norm_ffn, norm_kv, norm_final, w_qkv_a, w_o_a, w_kv_b, w_q_b, w_o_b, w_ffn_gate, w_ffn_up, ffn_conv_w, ffn_conv_b, w_ffn_down):
    n_p, seq, d = x_prompt.shape
    n_s, dec_seq, _ = x_sample.shape
    n_groups = len(DIL_GROUPS)
    width = w_o_a.shape[1]
    n_heads = width // HEAD_DIM
    d_ff = w_ffn_gate.shape[2]
    past_len = page_table.shape[1] * PAGE_SIZE
    assert w_qkv_a.shape[0] == 1 and w_q_b.shape[0] == 1, "one self-decoder and one cross-decoder layer"
    assert width == 1024 and seq % DIL_CHUNK_ROWS == 0 and seq % (SUBLANES * MOBA_BLOCK) == 0
    assert (n_s * dec_seq) % LANES == 0 and n_s % NEW_ROWS_PER_LANE_TILE == 0
    assert NEW_ROWS_PER_LANE_TILE * dec_seq == LANES and d_ff % FFN_CHUNK == 0
    assert all(dil & (dil - 1) == 0 and window // dil == BAND_BLOCK and BAND_BLOCK * dil <= DIL_CHUNK_ROWS
               for window, dil in DIL_GROUPS)

    wqkv = w_qkv_a[0].astype(BF16)
    wqkv_t = w_qkv_a[0].T.astype(BF16)
    wo_a = w_o_a[0].astype(BF16)
    wkv = w_kv_b.astype(BF16)
    wkv_t = w_kv_b.T.astype(BF16)
    wq_b = w_q_b[0].astype(BF16)
    wo_b = w_o_b[0].astype(BF16)
    wg = w_ffn_gate.astype(BF16)
    wu = w_ffn_up.astype(BF16)
    wd = w_ffn_down.astype(BF16)

    t_s = n_s * dec_seq
    xs = jnp.transpose(x_sample, (1, 0, 2)).reshape(t_s, d)
    pos_s = past_len + jnp.repeat(jnp.arange(dec_seq, dtype=jnp.int32), n_s)
    tab_s, _ = _rope_tables(pos_s)

    def to_head_major(a):
        c = a.shape[-1] // width
        a = jnp.transpose(a.reshape(dec_seq, n_s, c, n_heads, HEAD_DIM), (2, 1, 3, 0, 4))
        return jnp.pad(a, ((0, 0), (0, 0), (0, 0), (0, SAMPLE_Q_PAD - dec_seq), (0, 0)))

    def to_time_major(a):
        return jnp.transpose(a[:, :, :dec_seq], (2, 0, 1, 3)).reshape(t_s, width)

    def conv_prev(state):
        return jnp.transpose(state, (1, 0, 2)).reshape((CONV_WIDTH - 1) * n_s, d_ff)

    def conv_next(tail):
        return jnp.transpose(tail.reshape(CONV_WIDTH - 1, n_s, d_ff), (1, 0, 2))

    qkv_s = _norm_matmul(xs, norm_mix[0], wqkv, tab_s, rope_period=3, rope_count=2, tm=t_s)
    qkv_hm = to_head_major(qkv_s)
    qkv_t3 = jnp.transpose(qkv_s.reshape(dec_seq, n_s, n_groups * 3 * n_heads, HEAD_DIM), (2, 3, 1, 0))
    qkv_t3 = qkv_t3.reshape(n_groups * 3 * n_heads, HEAD_DIM, t_s)
    caches = ((cache_a_k0, cache_a_v0), (cache_a_k1, cache_a_v1), (cache_a_k2, cache_a_v2))

    t_p = n_p * seq
    xp = x_prompt.reshape(t_p, d)
    tab_p, tab_p_fm = _rope_tables(jnp.tile(jnp.arange(seq, dtype=jnp.int32), n_p))
    tm_mm = _row_tile(t_p, 1024)
    tm_p = _row_tile(seq, 512)

    qkv = _norm_matmul(xp, norm_mix[0], wqkv, tab_p, rope_period=3, rope_count=2, tm=tm_mm)
    outs, lses, pa = [], [], []
    for g, (window, _) in enumerate(DIL_GROUPS):
        o, lse = _dilated_attention_prompt(qkv, g, n_p, seq, width, n_groups)
        outs.append(o)
        lses.append(lse)
        keep = min(window, seq)
        k_t, v_t = _proj_feature_major(xp, norm_mix[0], wqkv_t, g * 3 + 1, tab_p_fm, n_p, seq, keep,
                                       tr=min(keep, 512))
        pa.append(jnp.transpose(k_t[None], (0, 1, 4, 2, 3)))
        pa.append(jnp.transpose(v_t[None], (0, 1, 4, 2, 3)))
    h = _merge_oproj(xp, wo_a, outs, lses, tm=tm_p)
    h, tail0 = _conv_ffn(h, None, norm_ffn[0], wg[0], wu[0], wd[0], ffn_conv_w[0], ffn_conv_b[0], None,
                         tm=tm_p, conv_shift=1, seq_len=seq)
    k_tok = _norm_matmul(h, norm_kv, wkv, tab_p, rope_period=1, rope_count=1, tm=tm_mm, n_out=width)
    kb_t, vb_t = _proj_feature_major(h, norm_kv, wkv_t, 0, tab_p_fm, n_p, seq, seq, tr=512)
    kmean = _block_mean(k_tok).reshape(n_p, seq // MOBA_BLOCK, width)
    qb = _norm_matmul(h, norm_mix[1], wq_b, tab_p, rope_period=1, rope_count=1, tm=tm_mm)
    rider_group = n_groups - 1
    n_moba_blocks = seq // MOBA_BLOCK
    moba_steps = n_p * n_moba_blocks * (n_moba_blocks + 1) // 2
    rider = None
    if n_s * (n_heads // RIDER_HEAD_CHUNK) <= moba_steps:
        rider = (qkv_hm, qkv_t3, caches[rider_group][0], caches[rider_group][1], rider_group, dec_seq,
                 RIDER_HEAD_CHUNK)
    else:
        rider_group = None
    o, *rider_results = _moba_prompt(qb, k_tok, vb_t, kmean, n_p, seq, width, rider=rider)
    h = _merge_oproj(h, wo_b, [o], [], tm=tm_p)
    h, tail1, y_p = _conv_ffn(h, None, norm_ffn[1], wg[1], wu[1], wd[1], ffn_conv_w[1], ffn_conv_b[1], norm_final,
                              tm=tm_p, conv_shift=1, seq_len=seq)
    y_p = y_p.reshape(n_p, seq, d)
    pb_k = jnp.transpose(kb_t, (0, 3, 1, 2))
    pb_v = jnp.transpose(vb_t, (0, 3, 1, 2))
    tiles_per_seq = seq // tm_p

    def prompt_conv_state(tail):
        last = tail.reshape(n_p, tiles_per_seq, SUBLANES, d_ff)[:, -1]
        return last[:, SUBLANES - (CONV_WIDTH - 1):]

    p_conv = jnp.stack([prompt_conv_state(tail0), prompt_conv_state(tail1)], axis=0)

    outs, lses, sa = [], [], []
    for g in range(n_groups):
        if g == rider_group:
            o, lse, ko, vo = _sample_window_results(rider_results)
        else:
            o, lse, ko, vo = _sample_window_attention(qkv_hm, qkv_t3, caches[g][0], caches[g][1], g, dec_seq)
        outs.append(to_time_major(o))
        lses.append(to_time_major(lse))
        sa += [ko, vo]
    hs = _merge_oproj(xs, wo_a, outs, lses, tm=t_s)
    hs, s_tail0 = _conv_ffn(hs, conv_prev(state_ffn_conv[0]), norm_ffn[0], wg[0], wu[0], wd[0], ffn_conv_w[0],
                            ffn_conv_b[0], None, tm=t_s, conv_shift=n_s, seq_len=dec_seq)
    kv_s = _norm_matmul(hs, norm_kv, wkv, tab_s, rope_period=2, rope_count=1, tm=t_s)
    qb_s = _norm_matmul(hs, norm_mix[1], wq_b, tab_s, rope_period=1, rope_count=1, tm=t_s)
    kv_hm = to_head_major(kv_s)
    o = _moba_sample(to_head_major(qb_s)[0], kv_hm[0], kv_hm[1], cache_b_k, cache_b_v, page_table, dec_seq)
    hs = _merge_oproj(hs, wo_b, [to_time_major(o)], [], tm=t_s)
    hs, s_tail1, y_s = _conv_ffn(hs, conv_prev(state_ffn_conv[1]), norm_ffn[1], wg[1], wu[1], wd[1], ffn_conv_w[1],
                                 ffn_conv_b[1], norm_final, tm=t_s, conv_shift=n_s, seq_len=dec_seq)
    y_s = jnp.transpose(y_s.reshape(dec_seq, n_s, d), (1, 0, 2))
    kv_s5 = jnp.transpose(kv_s.reshape(dec_seq, n_s, 2, n_heads, HEAD_DIM), (2, 1, 0, 3, 4))
    sb_k, sb_v = kv_s5[0], kv_s5[1]
    s_conv = jnp.stack([conv_next(s_tail0), conv_next(s_tail1)], axis=0)

    return (y_p, y_s, *pa, *sa, pb_k, pb_v, sb_k, sb_v, p_conv, s_conv)
```

```python
import functools

import jax
import jax.numpy as jnp
from jax import lax
from jax.experimental import pallas as pl
from jax.experimental.pallas import tpu as pltpu

F32 = jnp.float32
BF16 = jnp.bfloat16

HEAD_DIM = 64
DIL_GROUPS = ((128, 1), (512, 4), (2048, 16))
BAND_BLOCK = 128
MOBA_BLOCK = 256
MOBA_TOP_K = 3
PAGE_SIZE = 128
CONV_WIDTH = 3
ROPE_THETA = 10000.0
NORM_EPS = 1e-6
ATTN_SCALE = HEAD_DIM ** -0.5

LANES = 128
SUBLANES = 8
VMEM_LIMIT_BYTES = 56 * 1024 * 1024

HEADS_PER_VREG = LANES // HEAD_DIM
SAMPLE_Q_PAD = SUBLANES
NEG_BIG = -1e30
MASK_BIAS = -2e30


def _cparams(semantics):
    return pltpu.CompilerParams(dimension_semantics=semantics, vmem_limit_bytes=VMEM_LIMIT_BYTES)


def _nt_dot(a, b):
    return lax.dot_general(a, b, (((1,), (1,)), ((), ())), preferred_element_type=F32)


def _bdot(a, b, ca, cb):
    return lax.dot_general(a, b, (((ca,), (cb,)), ((0,), (0,))), preferred_element_type=F32)


def _split_bf16(x):
    hi = x.astype(BF16)
    lo = (x - hi.astype(F32)).astype(BF16)
    return hi, lo


def _dot_precise(a, b):
    a_hi, a_lo = _split_bf16(a)
    b_hi, b_lo = _split_bf16(b)
    dot = functools.partial(jnp.dot, preferred_element_type=F32)
    return dot(a_hi, b_hi) + (dot(a_hi, b_lo) + dot(a_lo, b_hi))


def _rmsnorm_val(x, gain):
    r = lax.rsqrt(jnp.mean(x * x, axis=-1, keepdims=True) + NORM_EPS)
    return (x * r) * gain


def _rope_tables(pos):
    half = HEAD_DIM // 2
    inv_freq = ROPE_THETA ** (-jnp.arange(half, dtype=F32) / half)
    ang = pos.astype(F32)[:, None] * inv_freq[None, :]
    cos, sin = jnp.cos(ang), jnp.sin(ang)
    zero = jnp.zeros_like(sin)
    reps = LANES // HEAD_DIM
    cos_t = jnp.tile(jnp.concatenate([cos, cos], axis=1), (1, reps))
    sin_lo = jnp.tile(jnp.concatenate([-sin, zero], axis=1), (1, reps))
    sin_hi = jnp.tile(jnp.concatenate([zero, sin], axis=1), (1, reps))
    return (cos_t, sin_lo, sin_hi), (cos.T, sin.T)


NORM_MM_CHUNK = 256


def _norm_mm_kernel(x_ref, g_ref, w_ref, cos_ref, slo_ref, shi_ref, o_ref, xn_ref, *, rope_period, rope_count):
    j = pl.program_id(1)

    @pl.when(j == 0)
    def _():
        xn_ref[...] = _rmsnorm_val(x_ref[...], g_ref[...]).astype(BF16)

    tn = o_ref.shape[1]
    half = HEAD_DIM // 2

    def chunks():
        for c in range(tn // NORM_MM_CHUNK):
            cols = slice(c * NORM_MM_CHUNK, (c + 1) * NORM_MM_CHUNK)
            yield c, jnp.dot(xn_ref[...], w_ref[:, cols], preferred_element_type=F32)

    def store_rope():
        cos, slo, shi = cos_ref[...], slo_ref[...], shi_ref[...]
        for c, y in chunks():
            for cc in range(NORM_MM_CHUNK // LANES):
                yc = y[:, cc * LANES:(cc + 1) * LANES]
                lo = c * NORM_MM_CHUNK + cc * LANES
                o_ref[:, lo:lo + LANES] = (
                    yc * cos + pltpu.roll(yc, LANES - half, 1) * slo + pltpu.roll(yc, half, 1) * shi)

    def store_plain():
        for c, y in chunks():
            o_ref[:, c * NORM_MM_CHUNK:(c + 1) * NORM_MM_CHUNK] = y

    if rope_count == rope_period:
        store_rope()
    else:
        is_rope = (j % rope_period) < rope_count
        pl.when(is_rope)(store_rope)
        pl.when(jnp.logical_not(is_rope))(store_plain)


def _norm_matmul(x, gain, w_bf16, tables, *, rope_period, rope_count, tm, n_out=None, tn=1024):
    t, d = x.shape
    n = w_bf16.shape[1] if n_out is None else n_out
    kern = functools.partial(_norm_mm_kernel, rope_period=rope_period, rope_count=rope_count)
    tab_spec = pl.BlockSpec((tm, LANES), lambda i, j: (i, 0))
    return pl.pallas_call(
        kern,
        grid=(t // tm, n // tn),
        in_specs=[
            pl.BlockSpec((tm, d), lambda i, j: (i, 0)),
            pl.BlockSpec((1, d), lambda i, j: (0, 0)),
            pl.BlockSpec((d, tn), lambda i, j: (0, j)),
            tab_spec, tab_spec, tab_spec,
        ],
        out_specs=pl.BlockSpec((tm, tn), lambda i, j: (i, j)),
        out_shape=jax.ShapeDtypeStruct((t, n), F32),
        scratch_shapes=[pltpu.VMEM((tm, d), BF16)],
        compiler_params=_cparams(("parallel", "arbitrary")),
        name="norm_matmul",
    )(x, gain.reshape(1, d), w_bf16, *tables)


def _proj_fm_kernel(x_ref, g_ref, wt_ref, cos_ref, sin_ref, k_ref, v_ref, xn_ref):
    j = pl.program_id(1)
    half = HEAD_DIM // 2

    @pl.when(j == 0)
    def _():
        xn_ref[...] = _rmsnorm_val(x_ref[...], g_ref[...]).astype(BF16)

    def project():
        y = _nt_dot(wt_ref[...], xn_ref[...])
        return y.reshape(y.shape[0] // HEAD_DIM, HEAD_DIM, y.shape[1])

    @pl.when(j == 0)
    def _():
        y = project()
        cos, sin = cos_ref[...], sin_ref[...]
        x1, x2 = y[:, :half, :], y[:, half:, :]
        k_ref[0] = jnp.concatenate([x1 * cos - x2 * sin, x2 * cos + x1 * sin], axis=1)

    @pl.when(j == 1)
    def _():
        v_ref[0] = project()


def _proj_feature_major(x, gain, wt_bf16, w_block, tables_fm, n_seq, seq_len, keep, *, tr):
    d = x.shape[1]
    width = 1024
    n_heads = width // HEAD_DIM
    tiles = keep // tr
    seq_tiles = seq_len // tr
    first = seq_tiles - tiles

    def rows(i):
        return (i // tiles) * seq_tiles + first + (i % tiles)

    out_spec = pl.BlockSpec((1, n_heads, HEAD_DIM, tr), lambda i, j: (i // tiles, 0, 0, i % tiles))
    out_sds = jax.ShapeDtypeStruct((n_seq, n_heads, HEAD_DIM, keep), F32)
    tab_spec = pl.BlockSpec((HEAD_DIM // 2, tr), lambda i, j: (0, rows(i)))
    return pl.pallas_call(
        _proj_fm_kernel,
        grid=(n_seq * tiles, 2),
        in_specs=[
            pl.BlockSpec((tr, d), lambda i, j: (rows(i), 0)),
            pl.BlockSpec((1, d), lambda i, j: (0, 0)),
            pl.BlockSpec((width, d), lambda i, j: (w_block + j, 0)),
            tab_spec, tab_spec,
        ],
        out_specs=[out_spec, out_spec],
        out_shape=[out_sds, out_sds],
        scratch_shapes=[pltpu.VMEM((tr, d), BF16)],
        compiler_params=_cparams(("parallel", "arbitrary")),
        name="proj_feature_major",
    )(x, gain.reshape(1, d), wt_bf16, *tables_fm)


DIL_CHUNK_ROWS = 2048
DIL_HEAD_LANES = LANES


def _dil_attn_kernel(q_ref, k_ref, v_ref, kp_ref, vp_ref, o_ref, l_ref, *, dil, n_back):
    i = pl.program_id(1)
    blk = BAND_BLOCK
    span = blk * dil
    n_units = q_ref.shape[0] // blk
    pair_w = HEADS_PER_VREG * blk
    ki = lax.broadcasted_iota(jnp.int32, (2 * blk, pair_w), 0)
    qi = (lax.broadcasted_iota(jnp.int32, (2 * blk, pair_w), 1) & (blk - 1)) + blk
    dist = qi - ki
    in_band = (dist >= 0) & (dist <= n_back)
    is_cur = ki >= blk
    first_head = lax.broadcasted_iota(jnp.int32, (blk, LANES), 1) < HEAD_DIM

    def rows(ref, start):
        if dil == 1:
            return ref[pl.ds(pl.multiple_of(start, blk), blk), :]
        return ref[pl.ds(start, blk, stride=dil), :]

    def unit_body(u, carry):
        sb = u // dil
        r = u % dil
        start = sb * span + r
        prev_here = sb > 0
        prev_start = jnp.maximum(start - span, 0)
        q2 = rows(q_ref, start) * ATTN_SCALE
        q_bd = jnp.concatenate([jnp.where(first_head, q2, 0.0), jnp.where(first_head, 0.0, q2)], axis=0).astype(BF16)
        k_prev = jnp.where(prev_here, rows(k_ref, prev_start), rows(kp_ref, r))
        v_prev = jnp.where(prev_here, rows(v_ref, prev_start), rows(vp_ref, r))
        k2 = jnp.concatenate([k_prev, rows(k_ref, start)], axis=0).astype(BF16)
        v2 = jnp.concatenate([v_prev, rows(v_ref, start)], axis=0).astype(BF16)
        mask = in_band & (is_cur | (prev_here | (i > 0)))
        s = jnp.where(mask, _nt_dot(k2, q_bd), -jnp.inf)
        m = jnp.max(s, axis=0, keepdims=True)
        e = jnp.exp(s - m)
        den = jnp.sum(e, axis=0, keepdims=True)
        prob = (e * (1.0 / den)).astype(BF16)
        lse = m + jnp.log(den)
        outs, lses = [], []
        for h in range(HEADS_PER_VREG):
            outs.append(lax.dot_general(prob[:, h * blk:(h + 1) * blk], v2[:, h * HEAD_DIM:(h + 1) * HEAD_DIM],
                                        (((0,), (0,)), ((), ())), preferred_element_type=F32))
            lses.append(jnp.broadcast_to(lse[:, h * blk:(h + 1) * blk], (HEAD_DIM, blk)))
        o_val = jnp.concatenate(outs, axis=1)
        l_val = jnp.concatenate(lses, axis=0).T
        if dil == 1:
            o_ref[pl.ds(pl.multiple_of(start, blk), blk), :] = o_val
            l_ref[pl.ds(pl.multiple_of(start, blk), blk), :] = l_val
        else:
            o_ref[pl.ds(start, blk, stride=dil), :] = o_val
            l_ref[pl.ds(start, blk, stride=dil), :] = l_val
        return carry

    lax.fori_loop(0, n_units, unit_body, 0, unroll=True)


def _dilated_attention_prompt(qkv, group, n_seq, seq_len, width, n_groups):
    window, dil = DIL_GROUPS[group]
    rows, hl = DIL_CHUNK_ROWS, DIL_HEAD_LANES
    span = BAND_BLOCK * dil
    chunks = seq_len // rows
    per_w = width // hl
    base = group * 3

    def cur(which):
        return pl.BlockSpec((rows, hl), lambda b, i, c: (b * chunks + i, (base + which) * per_w + c))

    def prev(which):
        return pl.BlockSpec(
            (span, hl),
            lambda b, i, c: (jnp.maximum((b * chunks + i) * (rows // span) - 1, 0), (base + which) * per_w + c))

    out_spec = pl.BlockSpec((rows, hl), lambda b, i, c: (b * chunks + i, c))
    out_sds = jax.ShapeDtypeStruct((n_seq * seq_len, width), F32)
    kern = functools.partial(_dil_attn_kernel, dil=dil, n_back=window // dil)
    return pl.pallas_call(
        kern,
        grid=(n_seq, chunks, per_w),
        in_specs=[cur(0), cur(1), cur(2), prev(1), prev(2)],
        out_specs=[out_spec, out_spec],
        out_shape=[out_sds, out_sds],
        compiler_params=_cparams(("parallel", "parallel", "parallel")),
        name="dilated_attn_prompt",
    )(qkv, qkv, qkv, qkv, qkv)


def _merge_oproj_kernel(*refs, n_groups):
    x_ref, w_ref = refs[0], refs[1]
    o_refs = refs[2:2 + n_groups]
    l_refs = refs[2 + n_groups:2 + 2 * n_groups]
    out_ref = refs[-1]
    if n_groups == 1:
        o = o_refs[0][...]
    else:
        ls = [r[...] for r in l_refs]
        m = functools.reduce(jnp.maximum, ls)
        es = [jnp.exp(l - m) for l in ls]
        den = functools.reduce(lambda a, b: a + b, es)
        o = None
        for e, o_ref in zip(es, o_refs):
            term = (e / den) * o_ref[...]
            o = term if o is None else o + term
    out_ref[...] = x_ref[...] + jnp.dot(o.astype(BF16), w_ref[...], preferred_element_type=F32)


def _merge_oproj(x, w_bf16, outs, lses, *, tm):
    t, d = x.shape
    width = w_bf16.shape[0]
    n_groups = len(outs)
    row_spec = pl.BlockSpec((tm, width), lambda i: (i, 0))
    kern = functools.partial(_merge_oproj_kernel, n_groups=n_groups)
    return pl.pallas_call(
        kern,
        grid=(t // tm,),
        in_specs=[pl.BlockSpec((tm, d), lambda i: (i, 0)), pl.BlockSpec((width, d), lambda i: (0, 0))]
        + [row_spec] * (n_groups + len(lses)),
        out_specs=pl.BlockSpec((tm, d), lambda i: (i, 0)),
        out_shape=jax.ShapeDtypeStruct((t, d), F32),
        compiler_params=_cparams(("parallel",)),
        name="merge_oproj",
    )(x, w_bf16, *outs, *lses)


FFN_CHUNK = 256


def _ffn_kernel(*refs, conv_shift, tiles_per_seq, final):
    it = iter(refs)
    h_ref, prev_ref, gn_ref, wg_ref, wu_ref, wd_ref, cw_ref, cb_ref = (next(it) for _ in range(8))
    gf_ref = next(it) if final else None
    out_ref, tail_ref = next(it), next(it)
    y_ref = next(it) if final else None
    xn_ref, act_ref = next(it), next(it)

    i = pl.program_id(0)
    tm = h_ref.shape[0]
    n_chunks = wg_ref.shape[1] // FFN_CHUNK
    x = h_ref[...]
    xn_ref[...] = _rmsnorm_val(x, gn_ref[...]).astype(BF16)
    if conv_shift == 1:
        xn_prev = _rmsnorm_val(prev_ref[...], gn_ref[...]).astype(BF16)
        has_prev = (i % tiles_per_seq) != 0
        row = lax.broadcasted_iota(jnp.int32, (tm, FFN_CHUNK), 0)

    def chunk_body(c, carry):
        off = pl.multiple_of(c * FFN_CHUNK, FFN_CHUNK)
        wg = wg_ref[:, pl.ds(off, FFN_CHUNK)]
        g = jnp.dot(xn_ref[...], wg, preferred_element_type=F32)
        u = jnp.dot(xn_ref[...], wu_ref[:, pl.ds(off, FFN_CHUNK)], preferred_element_type=F32)
        if conv_shift == 1:
            gp = jnp.dot(xn_prev, wg, preferred_element_type=F32)
            gp = jnp.where(has_prev, gp, 0.0)
            last, last2 = gp[SUBLANES - 1:SUBLANES, :], gp[SUBLANES - 2:SUBLANES - 1, :]
            g1 = jnp.where(row == 0, last, pltpu.roll(g, 1, 0))
            g2 = jnp.where(row == 0, last2, jnp.where(row == 1, last, pltpu.roll(g, 2, 0)))
            tail_ref[0, :, pl.ds(off, FFN_CHUNK)] = g[tm - SUBLANES:, :]
        else:
            ext = jnp.concatenate([prev_ref[:, pl.ds(off, FFN_CHUNK)], g], axis=0)
            g2 = ext[:tm, :]
            g1 = ext[conv_shift:conv_shift + tm, :]
            tail_ref[:, pl.ds(off, FFN_CHUNK)] = ext[tm:, :]
        cw = cw_ref[:, pl.ds(off, FFN_CHUNK)]
        conv = cb_ref[:, pl.ds(off, FFN_CHUNK)] + ((cw[0:1, :] * g2 + cw[1:2, :] * g1) + cw[2:3, :] * g)
        act = (conv * (1.0 / (1.0 + jnp.exp(-conv)))) * u
        act_ref[:, pl.ds(off, FFN_CHUNK)] = act.astype(BF16)
        return carry

    lax.fori_loop(0, n_chunks, chunk_body, 0, unroll=True)
    h_out = x + jnp.dot(act_ref[...], wd_ref[...], preferred_element_type=F32)
    out_ref[...] = h_out
    if final:
        y_ref[...] = _rmsnorm_val(h_out, gf_ref[...])


def _conv_ffn(h, prev, gain, wg, wu, wd, conv_w, conv_b, final_gain, *, tm, conv_shift, seq_len):
    t, d = h.shape
    d_ff = wg.shape[1]
    final = final_gain is not None
    const2 = lambda i: (0, 0)
    in_specs = [pl.BlockSpec((tm, d), lambda i: (i, 0))]
    args = [h]
    if conv_shift == 1:
        rows8 = tm // SUBLANES
        in_specs.append(pl.BlockSpec((SUBLANES, d), lambda i: (jnp.maximum(i * rows8 - 1, 0), 0)))
        args.append(h)
        tail_shape = jax.ShapeDtypeStruct((t // tm, SUBLANES, d_ff), F32)
        tail_spec = pl.BlockSpec((1, SUBLANES, d_ff), lambda i: (i, 0, 0))
        tiles_per_seq = seq_len // tm
    else:
        assert t == tm
        n_prev = (CONV_WIDTH - 1) * conv_shift
        in_specs.append(pl.BlockSpec((n_prev, d_ff), const2))
        args.append(prev)
        tail_shape = jax.ShapeDtypeStruct((n_prev, d_ff), F32)
        tail_spec = pl.BlockSpec((n_prev, d_ff), const2)
        tiles_per_seq = 1
    resident = pl.Buffered(1)
    in_specs += [
        pl.BlockSpec((1, d), const2),
        pl.BlockSpec((d, d_ff), const2, pipeline_mode=resident),
        pl.BlockSpec((d, d_ff), const2, pipeline_mode=resident),
        pl.BlockSpec((d_ff, d), const2, pipeline_mode=resident),
        pl.BlockSpec((CONV_WIDTH, d_ff), const2), pl.BlockSpec((1, d_ff), const2),
    ]
    args += [gain.reshape(1, d), wg, wu, wd, conv_w, conv_b.reshape(1, d_ff)]
    out_specs = [pl.BlockSpec((tm, d), lambda i: (i, 0)), tail_spec]
    out_shape = [jax.ShapeDtypeStruct((t, d), F32), tail_shape]
    if final:
        in_specs.append(pl.BlockSpec((1, d), const2))
        args.append(final_gain.reshape(1, d))
        out_specs.append(pl.BlockSpec((tm, d), lambda i: (i, 0)))
        out_shape.append(jax.ShapeDtypeStruct((t, d), F32))
    kern = functools.partial(_ffn_kernel, conv_shift=conv_shift, tiles_per_seq=tiles_per_seq, final=final)
    return pl.pallas_call(
        kern,
        grid=(t // tm,),
        in_specs=in_specs,
        out_specs=out_specs,
        out_shape=out_shape,
        scratch_shapes=[pltpu.VMEM((tm, d), BF16), pltpu.VMEM((tm, d_ff), BF16)],
        compiler_params=_cparams(("parallel",)),
        name="conv_ffn",
    )(*args)


def _block_mean_kernel(k_ref, o_ref):
    for r in range(o_ref.shape[0]):
        o_ref[r:r + 1, :] = jnp.mean(k_ref[r * MOBA_BLOCK:(r + 1) * MOBA_BLOCK, :], axis=0, keepdims=True)


def _block_mean(k):
    t, width = k.shape
    rows = SUBLANES * MOBA_BLOCK
    return pl.pallas_call(
        _block_mean_kernel,
        grid=(t // rows,),
        in_specs=[pl.BlockSpec((rows, width), lambda i: (i, 0))],
        out_specs=pl.BlockSpec((SUBLANES, width), lambda i: (i, 0)),
        out_shape=jax.ShapeDtypeStruct((t // MOBA_BLOCK, width), F32),
        compiler_params=_cparams(("parallel",)),
        name="moba_block_mean",
    )(k)


def _top_k_mask(gate, n_valid, k_sel, axis):
    nb = gate.shape[axis]
    blk_id = lax.broadcasted_iota(jnp.int32, gate.shape, axis)
    sel = jnp.zeros(gate.shape, F32)
    g = gate
    for kk in range(k_sel):
        mx = jnp.max(g, axis=axis, keepdims=True)
        idx = jnp.min(jnp.where(g == mx, blk_id, nb), axis=axis, keepdims=True)
        hit = blk_id == idx
        counts = jnp.where(kk < n_valid, 1.0, 0.0)
        sel = jnp.maximum(sel, jnp.where(hit, counts, 0.0))
        g = jnp.where(hit, -jnp.inf, g)
    return sel


def _moba_prompt_kernel(qi_tab, kj_tab, q_ref, k_ref, vt_ref, km_ref, *rest,
                        n_pairs, k_sel, steps_per_seq, rider_steps, rider_chunks, rider_cfg):
    n_rider_in, n_rider_out = (7, 4) if rider_steps else (0, 0)
    rider_in = rest[:n_rider_in]
    o_ref = rest[n_rider_in]
    rider_out = rest[n_rider_in + 1:n_rider_in + 1 + n_rider_out]
    qbd_ref, sel_ref, m_ref, l_ref, a_ref, acc_ref, s_ref, p_ref = rest[n_rider_in + 1 + n_rider_out:]
    step = pl.program_id(0)

    if rider_steps:
        @pl.when(step < rider_steps)
        def _():
            _sample_window_body(step // rider_chunks, *rider_in, *rider_out, **rider_cfg)

    t = step % steps_per_seq
    qi = qi_tab[t]
    kj = kj_tab[t]
    blk = MOBA_BLOCK
    nb = km_ref.shape[1]
    is_own = kj == qi

    pair_w = HEADS_PER_VREG * blk

    @pl.when(kj == 0)
    def _():
        m_ref[...] = jnp.full(m_ref.shape, NEG_BIG, F32)
        l_ref[...] = jnp.zeros_like(l_ref)
        acc_ref[...] = jnp.zeros_like(acc_ref)
        blk_id = lax.broadcasted_iota(jnp.int32, (nb, pair_w), 0)
        feat = lax.broadcasted_iota(jnp.int32, (LANES, blk), 0)
        for p in range(n_pairs):
            q_t = q_ref[:, p * LANES:(p + 1) * LANES].T
            q_bd = jnp.concatenate([jnp.where(feat < HEAD_DIM, q_t, 0.0),
                                    jnp.where(feat < HEAD_DIM, 0.0, q_t)], axis=1)
            qbd_ref[p] = (q_bd * ATTN_SCALE).astype(BF16)
            gate = _dot_precise(km_ref[0, :, p * LANES:(p + 1) * LANES], q_bd)
            gate = jnp.where(blk_id < qi, gate, -jnp.inf)
            sel_ref[p] = _top_k_mask(gate, qi, k_sel, 0)

    for p in range(n_pairs):
        s_ref[p] = jnp.dot(k_ref[:, p * LANES:(p + 1) * LANES].astype(BF16), qbd_ref[p],
                           preferred_element_type=F32)

    @pl.when(is_own)
    def _():
        key = lax.broadcasted_iota(jnp.int32, (blk, pair_w), 0)
        qry = lax.broadcasted_iota(jnp.int32, (blk, pair_w), 1) & (blk - 1)
        bias = jnp.where(key > qry, MASK_BIAS, 0.0)
        for p in range(n_pairs):
            s_ref[p] = s_ref[p] + bias

    takes_part = jnp.where(is_own, 1.0, sel_ref[:, pl.ds(kj, 1), :]) > 0.5
    s = s_ref[...]
    m_old = m_ref[...]
    m_new = jnp.where(takes_part, jnp.maximum(m_old, jnp.max(s, axis=1, keepdims=True)), m_old)
    alpha = jnp.exp(m_old - m_new)
    prob = jnp.exp(s - jnp.where(takes_part, m_new, -NEG_BIG))
    l_ref[...] = alpha * l_ref[...] + jnp.sum(prob, axis=1, keepdims=True)
    m_ref[...] = m_new
    a_ref[...] = alpha
    p_ref[...] = prob.astype(BF16)
    for h in range(n_pairs * HEADS_PER_VREG):
        p, c = divmod(h, HEADS_PER_VREG)
        qs = slice(c * blk, (c + 1) * blk)
        acc_ref[h] = a_ref[p, :, qs] * acc_ref[h] + jnp.dot(vt_ref[0, h].astype(BF16), p_ref[p, :, qs],
                                                             preferred_element_type=F32)

    @pl.when(is_own)
    def _():
        for p in range(n_pairs):
            o_t = jnp.concatenate(
                [acc_ref[p * HEADS_PER_VREG + c] / l_ref[p, :, c * blk:(c + 1) * blk] for c in range(HEADS_PER_VREG)],
                axis=0)
            o_ref[:, p * LANES:(p + 1) * LANES] = o_t.T


def _moba_prompt(q, k, v_t, kmean, n_seq, seq_len, width, rider=None):
    nb = seq_len // MOBA_BLOCK
    n_heads = width // HEAD_DIM
    pairs = [(a, b) for a in range(nb) for b in range(a + 1)]
    steps = len(pairs)
    qi_tab = jnp.asarray([p[0] for p in pairs], jnp.int32)
    kj_tab = jnp.asarray([p[1] for p in pairs], jnp.int32)
    n_pairs = width // LANES
    pair_w = HEADS_PER_VREG * MOBA_BLOCK
    plan = None
    if rider is not None:
        def wrap(f):
            def index_map(s, qt, kt):
                r = jnp.minimum(s, plan_steps - 1)
                return f(r // plan_chunks, r % plan_chunks)
            return index_map

        plan = _sample_window_plan(*rider, wrap)
        plan_chunks = plan["chunks"]
        plan_steps = plan["n_seq"] * plan_chunks
        assert plan_steps <= n_seq * steps
    kern = functools.partial(
        _moba_prompt_kernel, n_pairs=n_pairs, k_sel=min(MOBA_TOP_K, nb), steps_per_seq=steps,
        rider_steps=plan_steps if plan else 0, rider_chunks=plan_chunks if plan else 1,
        rider_cfg=plan["cfg"] if plan else None)
    o_spec = pl.BlockSpec((MOBA_BLOCK, width), lambda s, qt, kt: ((s // steps) * nb + qt[s % steps], 0))
    o_sds = jax.ShapeDtypeStruct((n_seq * seq_len, width), F32)
    grid_spec = pltpu.PrefetchScalarGridSpec(
        num_scalar_prefetch=2,
        grid=(n_seq * steps,),
        in_specs=[
            pl.BlockSpec((MOBA_BLOCK, width), lambda s, qt, kt: ((s // steps) * nb + qt[s % steps], 0)),
            pl.BlockSpec((MOBA_BLOCK, width), lambda s, qt, kt: ((s // steps) * nb + kt[s % steps], 0)),
            pl.BlockSpec((1, n_heads, HEAD_DIM, MOBA_BLOCK), lambda s, qt, kt: (s // steps, 0, 0, kt[s % steps])),
            pl.BlockSpec((1, nb, width), lambda s, qt, kt: (s // steps, 0, 0)),
        ] + (plan["in_specs"] if plan else []),
        out_specs=[o_spec] + (plan["out_specs"] if plan else []),
        scratch_shapes=[
            pltpu.VMEM((n_pairs, LANES, pair_w), BF16),
            pltpu.VMEM((n_pairs, nb, pair_w), F32),
            pltpu.VMEM((n_pairs, 1, pair_w), F32),
            pltpu.VMEM((n_pairs, 1, pair_w), F32),
            pltpu.VMEM((n_pairs, 1, pair_w), F32),
            pltpu.VMEM((n_heads, HEAD_DIM, MOBA_BLOCK), F32),
            pltpu.VMEM((n_pairs, MOBA_BLOCK, pair_w), F32),
            pltpu.VMEM((n_pairs, MOBA_BLOCK, pair_w), BF16),
        ],
    )
    return pl.pallas_call(
        kern,
        grid_spec=grid_spec,
        out_shape=[o_sds] + (plan["out_shape"] if plan else []),
        compiler_params=_cparams(("arbitrary",)),
        name="moba_prompt",
    )(qi_tab, kj_tab, q, k, v_t, kmean, *(plan["args"] if plan else []))


SAMPLE_STEP_BYTES = 2 * 1024 * 1024
RIDER_HEAD_CHUNK = 2
NEW_ROWS_PER_LANE_TILE = 32


def _sample_window_kernel(*refs, **cfg):
    _sample_window_body(pl.program_id(0), *refs, **cfg)


def _sample_window_body(n, q_ref, kn_ref, vn_ref, knt_ref, vnt_ref, kc_ref, vc_ref,
                        ko_ref, vo_ref, o_ref, l_ref, *, window, dil, n_new):
    hc, hd, n_buf = kc_ref.shape[2:]
    qp = q_ref.shape[3]
    shift = (LANES - n_new - (n % NEW_ROWS_PER_LANE_TILE) * n_new) % LANES
    lane = lax.broadcasted_iota(jnp.int32, (hc * hd, LANES), 1)
    is_new_lane = lane >= LANES - n_new

    for c_ref, nt_ref, out_ref in ((kc_ref, knt_ref, ko_ref), (vc_ref, vnt_ref, vo_ref)):
        rolled = pltpu.roll(c_ref[0, 0].reshape(hc * hd, n_buf), n_buf - n_new, 1)
        new_t = pltpu.roll(nt_ref[...].reshape(hc * hd, LANES), shift, 1)
        if n_buf > LANES:
            out_ref[0, 0, :, :, :n_buf - LANES] = rolled[:, :n_buf - LANES].reshape(hc, hd, n_buf - LANES)
        last = jnp.where(is_new_lane, new_t, rolled[:, n_buf - LANES:])
        out_ref[0, 0, :, :, n_buf - LANES:] = last.reshape(hc, hd, LANES)

    q_idx = lax.broadcasted_iota(jnp.int32, (qp, n_buf), 0)
    r_idx = lax.broadcasted_iota(jnp.int32, (qp, n_buf), 1)
    dist_c = n_buf + q_idx - r_idx
    dil_mask = dil - 1
    valid_c = ((dist_c & dil_mask) == 0) & (dist_c <= window)
    qn_idx = lax.broadcasted_iota(jnp.int32, (qp, qp), 0)
    jn_idx = lax.broadcasted_iota(jnp.int32, (qp, qp), 1)
    dist_n = qn_idx - jn_idx
    valid_n = (dist_n >= 0) & ((dist_n & dil_mask) == 0) & (dist_n <= window) & ((jn_idx < n_new) | (jn_idx == qn_idx))

    q = q_ref[0, 0].astype(BF16)
    s_c = _bdot(q, kc_ref[0, 0].astype(BF16), 2, 1) * ATTN_SCALE
    s_c = jnp.where(valid_c, s_c, -jnp.inf)
    s_n = _bdot(q, kn_ref[0, 0].astype(BF16), 2, 2) * ATTN_SCALE
    s_n = jnp.where(valid_n, s_n, -jnp.inf)
    m = jnp.maximum(jnp.max(s_c, axis=-1, keepdims=True), jnp.max(s_n, axis=-1, keepdims=True))
    e_c = jnp.exp(s_c - m)
    e_n = jnp.exp(s_n - m)
    den = jnp.sum(e_c, axis=-1, keepdims=True) + jnp.sum(e_n, axis=-1, keepdims=True)
    o = _bdot(e_c.astype(BF16), vc_ref[0, 0].astype(BF16), 2, 2)
    o = o + _bdot(e_n.astype(BF16), vn_ref[0, 0].astype(BF16), 2, 1)
    o_ref[0] = o / den
    l_ref[0] = jnp.broadcast_to(m + jnp.log(den), o.shape)


def _sample_window_attention(qkv_hm, qkv_t3, cache_k, cache_v, group, n_new):
    _, n_seq, n_buf, n_heads, hd = cache_k.shape
    hc = max(1, min(n_heads, SAMPLE_STEP_BYTES // (hd * n_buf * 4)))
    plan = _sample_window_plan(qkv_hm, qkv_t3, cache_k, cache_v, group, n_new, hc, lambda f: f)
    kern = functools.partial(_sample_window_kernel, **plan["cfg"])
    results = pl.pallas_call(
        kern,
        grid=(n_seq, plan["chunks"]),
        in_specs=plan["in_specs"],
        out_specs=plan["out_specs"],
        out_shape=plan["out_shape"],
        compiler_params=_cparams(("parallel", "parallel")),
        name="sample_window_attn",
    )(*plan["args"])
    return _sample_window_results(results)


def _sample_window_results(results):
    ko, vo, o, lse = results
    back = (0, 1, 4, 2, 3)
    return o, lse, jnp.transpose(ko, back), jnp.transpose(vo, back)


def _sample_window_plan(qkv_hm, qkv_t3, cache_k, cache_v, group, n_new, hc, wrap):
    window, dil = DIL_GROUPS[group]
    _, n_seq, n_buf, n_heads, hd = cache_k.shape
    qp = qkv_hm.shape[3]
    chunks = n_heads // hc
    ck = jnp.transpose(cache_k, (0, 1, 3, 4, 2))
    cv = jnp.transpose(cache_v, (0, 1, 3, 4, 2))
    base = group * 3

    def hm_spec(which):
        return pl.BlockSpec((1, 1, hc, qp, hd), wrap(lambda n, c: (base + which, n, c, 0, 0)))

    def t_spec(which):
        return pl.BlockSpec((hc, hd, LANES),
                            wrap(lambda n, c: ((base + which) * chunks + c, 0, n // NEW_ROWS_PER_LANE_TILE)))

    cache_spec = pl.BlockSpec((1, 1, hc, hd, n_buf), wrap(lambda n, c: (0, n, c, 0, 0)))
    out_spec = pl.BlockSpec((1, hc, qp, hd), wrap(lambda n, c: (n, c, 0, 0)))
    out_sds = jax.ShapeDtypeStruct((n_seq, n_heads, qp, hd), F32)
    return dict(
        args=[qkv_hm, qkv_hm, qkv_hm, qkv_t3, qkv_t3, ck, cv],
        in_specs=[hm_spec(0), hm_spec(1), hm_spec(2), t_spec(1), t_spec(2), cache_spec, cache_spec],
        out_specs=[cache_spec, cache_spec, out_spec, out_spec],
        out_shape=[jax.ShapeDtypeStruct(ck.shape, F32), jax.ShapeDtypeStruct(cv.shape, F32), out_sds, out_sds],
        cfg=dict(window=window, dil=dil, n_new=n_new), chunks=chunks, n_seq=n_seq)


def _moba_sample_kernel(pt_ref, q_ref, kn_ref, vn_ref, *refs, n_new, k_sel, n_pages, pages_per_block):
    k_refs, v_refs, o_ref = refs[:n_pages], refs[n_pages:2 * n_pages], refs[2 * n_pages]
    n_heads, qp, _ = q_ref.shape[1:]
    n_blocks = n_pages // pages_per_block
    q = q_ref[0].astype(BF16)
    raw = [_bdot(q, k_ref[0].astype(BF16), 2, 1) for k_ref in k_refs]
    blk_lane = lax.broadcasted_iota(jnp.int32, (n_heads, qp, LANES), 2)
    gate = jnp.full((n_heads, qp, LANES), -jnp.inf, F32)
    for bb in range(n_blocks):
        pages = raw[bb * pages_per_block:(bb + 1) * pages_per_block]
        total = functools.reduce(lambda a, b: a + b, [jnp.sum(r, axis=-1, keepdims=True) for r in pages])
        gate = jnp.where(blk_lane == bb, total / MOBA_BLOCK, gate)
    picked = _top_k_mask(gate, n_blocks, k_sel, 2)
    takes_part = [jnp.max(jnp.where(blk_lane == bb, picked, 0.0), axis=-1, keepdims=True) > 0.5
                  for bb in range(n_blocks)]

    qn_idx = lax.broadcasted_iota(jnp.int32, (qp, qp), 0)
    jn_idx = lax.broadcasted_iota(jnp.int32, (qp, qp), 1)
    valid_own = (jn_idx <= qn_idx) & ((jn_idx < n_new) | (jn_idx == qn_idx))
    s_own = jnp.where(valid_own, _bdot(q, kn_ref[0].astype(BF16), 2, 2) * ATTN_SCALE, -jnp.inf)
    m_all = jnp.max(s_own, axis=-1, keepdims=True)
    for j, r in enumerate(raw):
        page_max = jnp.max(r, axis=-1, keepdims=True) * ATTN_SCALE
        m_all = jnp.maximum(m_all, jnp.where(takes_part[j // pages_per_block], page_max, NEG_BIG))
    e_own = jnp.exp(s_own - m_all)
    den = jnp.sum(e_own, axis=-1, keepdims=True)
    num = _bdot(e_own.astype(BF16), vn_ref[0].astype(BF16), 2, 1)
    for j, (r, v_ref) in enumerate(zip(raw, v_refs)):
        e = jnp.where(takes_part[j // pages_per_block], jnp.exp(r * ATTN_SCALE - m_all), 0.0)
        den = den + jnp.sum(e, axis=-1, keepdims=True)
        num = num + _bdot(e.astype(BF16), v_ref[0].astype(BF16), 2, 2)
    o_ref[0] = num / den


def _moba_sample(q_hm, k_hm, v_hm, cache_k, cache_v, page_table, n_new):
    n_seq, n_pages = page_table.shape
    _, n_heads, qp, hd = q_hm.shape
    pages_per_block = MOBA_BLOCK // PAGE_SIZE
    assert pages_per_block == 2 and n_pages % pages_per_block == 0 and n_new <= SAMPLE_Q_PAD
    n_blocks = n_pages // pages_per_block
    ck = jnp.transpose(cache_k, (0, 2, 3, 1))
    cv = jnp.transpose(cache_v, (0, 2, 3, 1))
    pt = page_table.reshape(-1).astype(jnp.int32)

    def page_spec(j):
        return pl.BlockSpec((1, n_heads, hd, PAGE_SIZE), lambda n, pt_ref: (pt_ref[n * n_pages + j], 0, 0, 0))

    seq_spec = pl.BlockSpec((1, n_heads, qp, hd), lambda n, pt_ref: (n, 0, 0, 0))
    page_specs = [page_spec(j) for j in range(n_pages)]
    kern = functools.partial(_moba_sample_kernel, n_new=n_new, k_sel=min(MOBA_TOP_K, n_blocks + 1),
                             n_pages=n_pages, pages_per_block=pages_per_block)
    grid_spec = pltpu.PrefetchScalarGridSpec(
        num_scalar_prefetch=1,
        grid=(n_seq,),
        in_specs=[seq_spec, seq_spec, seq_spec] + page_specs + page_specs,
        out_specs=seq_spec,
    )
    return pl.pallas_call(
        kern,
        grid_spec=grid_spec,
        out_shape=jax.ShapeDtypeStruct((n_seq, n_heads, qp, hd), F32),
        compiler_params=_cparams(("parallel",)),
        name="moba_sample",
    )(pt, q_hm, k_hm, v_hm, *([ck] * n_pages), *([cv] * n_pages))


def _row_tile(t, cap):
    tm = min(t, cap)
    assert t % tm == 0
    return tm


def kernel(x_prompt, x_sample, cache_a_k0, cache_a_v0, cache_a_k1, cache_a_v1, cache_a_k2, cache_a_v2, cache_b_k, cache_b_v, state_ffn_conv, page_table, norm_mix, norm_ffn, norm_kv, norm_final, w_qkv_a, w_o_a, w_kv_b, w_q_b, w_o_b, w_ffn_gate, w_ffn_up, ffn_conv_w, ffn_conv_b, w_ffn_down):
    n_p, seq, d = x_prompt.shape
    n_s, dec_seq, _ = x_sample.shape
    n_groups = len(DIL_GROUPS)
    width = w_o_a.shape[1]
    n_heads = width // HEAD_DIM
    d_ff = w_ffn_gate.shape[2]
    past_len = page_table.shape[1] * PAGE_SIZE
    assert w_qkv_a.shape[0] == 1 and w_q_b.shape[0] == 1, "one self-decoder and one cross-decoder layer"
    assert width == 1024 and seq % DIL_CHUNK_ROWS == 0 and seq % (SUBLANES * MOBA_BLOCK) == 0
    assert (n_s * dec_seq) % LANES == 0 and n_s % NEW_ROWS_PER_LANE_TILE == 0
    assert NEW_ROWS_PER_LANE_TILE * dec_seq == LANES and d_ff % FFN_CHUNK == 0
    assert all(dil & (dil - 1) == 0 and window // dil == BAND_BLOCK and BAND_BLOCK * dil <= DIL_CHUNK_ROWS
               for window, dil in DIL_GROUPS)

    wqkv = w_qkv_a[0].astype(BF16)
    wqkv_t = w_qkv_a[0].T.astype(BF16)
    wo_a = w_o_a[0].astype(BF16)
    wkv = w_kv_b.astype(BF16)
    wkv_t = w_kv_b.T.astype(BF16)
    wq_b = w_q_b[0].astype(BF16)
    wo_b = w_o_b[0].astype(BF16)
    wg = w_ffn_gate.astype(BF16)
    wu = w_ffn_up.astype(BF16)
    wd = w_ffn_down.astype(BF16)

    t_s = n_s * dec_seq
    xs = jnp.transpose(x_sample, (1, 0, 2)).reshape(t_s, d)
    pos_s = past_len + jnp.repeat(jnp.arange(dec_seq, dtype=jnp.int32), n_s)
    tab_s, _ = _rope_tables(pos_s)

    def to_head_major(a):
        c = a.shape[-1] // width
        a = jnp.transpose(a.reshape(dec_seq, n_s, c, n_heads, HEAD_DIM), (2, 1, 3, 0, 4))
        return jnp.pad(a, ((0, 0), (0, 0), (0, 0), (0, SAMPLE_Q_PAD - dec_seq), (0, 0)))

    def to_time_major(a):
        return jnp.transpose(a[:, :, :dec_seq], (2, 0, 1, 3)).reshape(t_s, width)

    def conv_prev(state):
        return jnp.transpose(state, (1, 0, 2)).reshape((CONV_WIDTH - 1) * n_s, d_ff)

    def conv_next(tail):
        return jnp.transpose(tail.reshape(CONV_WIDTH - 1, n_s, d_ff), (1, 0, 2))

    qkv_s = _norm_matmul(xs, norm_mix[0], wqkv, tab_s, rope_period=3, rope_count=2, tm=t_s)
    qkv_hm = to_head_major(qkv_s)
    qkv_t3 = jnp.transpose(qkv_s.reshape(dec_seq, n_s, n_groups * 3 * n_heads, HEAD_DIM), (2, 3, 1, 0))
    qkv_t3 = qkv_t3.reshape(n_groups * 3 * n_heads, HEAD_DIM, t_s)
    caches = ((cache_a_k0, cache_a_v0), (cache_a_k1, cache_a_v1), (cache_a_k2, cache_a_v2))

    t_p = n_p * seq
    xp = x_prompt.reshape(t_p, d)
    tab_p, tab_p_fm = _rope_tables(jnp.tile(jnp.arange(seq, dtype=jnp.int32), n_p))
    tm_mm = _row_tile(t_p, 1024)
    tm_p = _row_tile(seq, 512)

    qkv = _norm_matmul(xp, norm_mix[0], wqkv, tab_p, rope_period=3, rope_count=2, tm=tm_mm)
    outs, lses, pa = [], [], []
    for g, (window, _) in enumerate(DIL_GROUPS):
        o, lse = _dilated_attention_prompt(qkv, g, n_p, seq, width, n_groups)
        outs.append(o)
        lses.append(lse)
        keep = min(window, seq)
        k_t, v_t = _proj_feature_major(xp, norm_mix[0], wqkv_t, g * 3 + 1, tab_p_fm, n_p, seq, keep,
                                       tr=min(keep, 512))
        pa.append(jnp.transpose(k_t[None], (0, 1, 4, 2, 3)))
        pa.append(jnp.transpose(v_t[None], (0, 1, 4, 2, 3)))
    h = _merge_oproj(xp, wo_a, outs, lses, tm=tm_p)
    h, tail0 = _conv_ffn(h, None, norm_ffn[0], wg[0], wu[0], wd[0], ffn_conv_w[0], ffn_conv_b[0], None,
                         tm=tm_p, conv_shift=1, seq_len=seq)
    k_tok = _norm_matmul(h, norm_kv, wkv, tab_p, rope_period=1, rope_count=1, tm=tm_mm, n_out=width)
    kb_t, vb_t = _proj_feature_major(h, norm_kv, wkv_t, 0, tab_p_fm, n_p, seq, seq, tr=512)
    kmean = _block_mean(k_tok).reshape(n_p, seq // MOBA_BLOCK, width)
    qb = _norm_matmul(h, norm_mix[1], wq_b, tab_p, rope_period=1, rope_count=1, tm=tm_mm)
    rider_group = n_groups - 1
    n_moba_blocks = seq // MOBA_BLOCK
    moba_steps = n_p * n_moba_blocks * (n_moba_blocks + 1) // 2
    rider = None
    if n_s * (n_heads // RIDER_HEAD_CHUNK) <= moba_steps:
        rider = (qkv_hm, qkv_t3, caches[rider_group][0], caches[rider_group][1], rider_group, dec_seq,
                 RIDER_HEAD_CHUNK)
    else:
        rider_group = None
    o, *rider_results = _moba_prompt(qb, k_tok, vb_t, kmean, n_p, seq, width, rider=rider)
    h = _merge_oproj(h, wo_b, [o], [], tm=tm_p)
    h, tail1, y_p = _conv_ffn(h, None, norm_ffn[1], wg[1], wu[1], wd[1], ffn_conv_w[1], ffn_conv_b[1], norm_final,
                              tm=tm_p, conv_shift=1, seq_len=seq)
    y_p = y_p.reshape(n_p, seq, d)
    pb_k = jnp.transpose(kb_t, (0, 3, 1, 2))
    pb_v = jnp.transpose(vb_t, (0, 3, 1, 2))
    tiles_per_seq = seq // tm_p

    def prompt_conv_state(tail):
        last = tail.reshape(n_p, tiles_per_seq, SUBLANES, d_ff)[:, -1]
        return last[:, SUBLANES - (CONV_WIDTH - 1):]

    p_conv = jnp.stack([prompt_conv_state(tail0), prompt_conv_state(tail1)], axis=0)

    outs, lses, sa = [], [], []
    for g in range(n_groups):
        if g == rider_group:
            o, lse, ko, vo = _sample_window_results(rider_results)
        else:
            o, lse, ko, vo = _sample_window_attention(qkv_hm, qkv_t3, caches[g][0], caches[g][1], g, dec_seq)
        outs.append(to_time_major(o))
        lses.append(to_time_major(lse))
        sa += [ko, vo]
    hs = _merge_oproj(xs, wo_a, outs, lses, tm=t_s)
    hs, s_tail0 = _conv_ffn(hs, conv_prev(state_ffn_conv[0]), norm_ffn[0], wg[0], wu[0], wd[0], ffn_conv_w[0],
                            ffn_conv_b[0], None, tm=t_s, conv_shift=n_s, seq_len=dec_seq)
    kv_s = _norm_matmul(hs, norm_kv, wkv, tab_s, rope_period=2, rope_count=1, tm=t_s)
    qb_s = _norm_matmul(hs, norm_mix[1], wq_b, tab_s, rope_period=1, rope_count=1, tm=t_s)
    kv_hm = to_head_major(kv_s)
    o = _moba_sample(to_head_major(qb_s)[0], kv_hm[0], kv_hm[1], cache_b_k, cache_b_v, page_table, dec_seq)
    hs = _merge_oproj(hs, wo_b, [to_time_major(o)], [], tm=t_s)
    hs, s_tail1, y_s = _conv_ffn(hs, conv_prev(state_ffn_conv[1]), norm_ffn[1], wg[1], wu[1], wd[1], ffn_conv_w[1],
                                 ffn_conv_b[1], norm_final, tm=t_s, conv_shift=n_s, seq_len=dec_seq)
    y_s = jnp.transpose(y_s.reshape(dec_seq, n_s, d), (1, 0, 2))
    kv_s5 = jnp.transpose(kv_s.reshape(dec_seq, n_s, 2, n_heads, HEAD_DIM), (2, 1, 0, 3, 4))
    sb_k, sb_v = kv_s5[0], kv_s5[1]
    s_conv = jnp.stack([conv_next(s_tail0), conv_next(s_tail1)], axis=0)

    return (y_p, y_s, *pa, *sa, pb_k, pb_v, sb_k, sb_v, p_conv, s_conv)
```

```python
import functools

import jax
import jax.numpy as jnp
from jax import lax
from jax.experimental import pallas as pl
from jax.experimental.pallas import tpu as pltpu

F32 = jnp.float32
BF16 = jnp.bfloat16

HEAD_DIM = 64
DIL_GROUPS = ((128, 1), (512, 4), (2048, 16))
BAND_BLOCK = 128
MOBA_BLOCK = 256
MOBA_TOP_K = 3
PAGE_SIZE = 128
CONV_WIDTH = 3
ROPE_THETA = 10000.0
NORM_EPS = 1e-6
ATTN_SCALE = HEAD_DIM ** -0.5

LANES = 128
SUBLANES = 8
VMEM_LIMIT_BYTES = 56 * 1024 * 1024

HEADS_PER_VREG = LANES // HEAD_DIM
SAMPLE_Q_PAD = SUBLANES
NEG_BIG = -1e30
MASK_BIAS = -2e30


def _cparams(semantics):
    return pltpu.CompilerParams(dimension_semantics=semantics, vmem_limit_bytes=VMEM_LIMIT_BYTES)


def _nt_dot(a, b):
    return lax.dot_general(a, b, (((1,), (1,)), ((), ())), preferred_element_type=F32)


def _bdot(a, b, ca, cb):
    return lax.dot_general(a, b, (((ca,), (cb,)), ((0,), (0,))), preferred_element_type=F32)


def _split_bf16(x):
    hi = x.astype(BF16)
    lo = (x - hi.astype(F32)).astype(BF16)
    return hi, lo


def _dot_precise(a, b):
    a_hi, a_lo = _split_bf16(a)
    b_hi, b_lo = _split_bf16(b)
    dot = functools.partial(jnp.dot, preferred_element_type=F32)
    return dot(a_hi, b_hi) + (dot(a_hi, b_lo) + dot(a_lo, b_hi))


def _rmsnorm_val(x, gain):
    r = lax.rsqrt(jnp.mean(x * x, axis=-1, keepdims=True) + NORM_EPS)
    return (x * r) * gain


def _rope_tables(pos):
    half = HEAD_DIM // 2
    inv_freq = ROPE_THETA ** (-jnp.arange(half, dtype=F32) / half)
    ang = pos.astype(F32)[:, None] * inv_freq[None, :]
    cos, sin = jnp.cos(ang), jnp.sin(ang)
    zero = jnp.zeros_like(sin)
    reps = LANES // HEAD_DIM
    cos_t = jnp.tile(jnp.concatenate([cos, cos], axis=1), (1, reps))
    sin_lo = jnp.tile(jnp.concatenate([-sin, zero], axis=1), (1, reps))
    sin_hi = jnp.tile(jnp.concatenate([zero, sin], axis=1), (1, reps))
    return (cos_t, sin_lo, sin_hi), (cos.T, sin.T)


NORM_MM_CHUNK = 256


def _norm_mm_kernel(x_ref, g_ref, w_ref, cos_ref, slo_ref, shi_ref, o_ref, xn_ref, *, rope_period, rope_count):
    j = pl.program_id(1)

    @pl.when(j == 0)
    def _():
        xn_ref[...] = _rmsnorm_val(x_ref[...], g_ref[...]).astype(BF16)

    tn = o_ref.shape[1]
    half = HEAD_DIM // 2

    def chunks():
        for c in range(tn // NORM_MM_CHUNK):
            cols = slice(c * NORM_MM_CHUNK, (c + 1) * NORM_MM_CHUNK)
            yield c, jnp.dot(xn_ref[...], w_ref[:, cols], preferred_element_type=F32)

    def store_rope():
        cos, slo, shi = cos_ref[...], slo_ref[...], shi_ref[...]
        for c, y in chunks():
            for cc in range(NORM_MM_CHUNK // LANES):
                yc = y[:, cc * LANES:(cc + 1) * LANES]
                lo = c * NORM_MM_CHUNK + cc * LANES
                o_ref[:, lo:lo + LANES] = (
                    yc * cos + pltpu.roll(yc, LANES - half, 1) * slo + pltpu.roll(yc, half, 1) * shi)

    def store_plain():
        for c, y in chunks():
            o_ref[:, c * NORM_MM_CHUNK:(c + 1) * NORM_MM_CHUNK] = y

    if rope_count == rope_period:
        store_rope()
    else:
        is_rope = (j % rope_period) < rope_count
        pl.when(is_rope)(store_rope)
        pl.when(jnp.logical_not(is_rope))(store_plain)


def _norm_matmul(x, gain, w_bf16, tables, *, rope_period, rope_count, tm, n_out=None, tn=1024):
    t, d = x.shape
    n = w_bf16.shape[1] if n_out is None else n_out
    kern = functools.partial(_norm_mm_kernel, rope_period=rope_period, rope_count=rope_count)
    tab_spec = pl.BlockSpec((tm, LANES), lambda i, j: (i, 0))
    return pl.pallas_call(
        kern,
        grid=(t // tm, n // tn),
        in_specs=[
            pl.BlockSpec((tm, d), lambda i, j: (i, 0)),
            pl.BlockSpec((1, d), lambda i, j: (0, 0)),
            pl.BlockSpec((d, tn), lambda i, j: (0, j)),
            tab_spec, tab_spec, tab_spec,
        ],
        out_specs=pl.BlockSpec((tm, tn), lambda i, j: (i, j)),
        out_shape=jax.ShapeDtypeStruct((t, n), F32),
        scratch_shapes=[pltpu.VMEM((tm, d), BF16)],
        compiler_params=_cparams(("parallel", "arbitrary")),
        name="norm_matmul",
    )(x, gain.reshape(1, d), w_bf16, *tables)


def _proj_fm_kernel(x_ref, g_ref, wt_ref, cos_ref, sin_ref, k_ref, v_ref, xn_ref):
    j = pl.program_id(1)
    half = HEAD_DIM // 2

    @pl.when(j == 0)
    def _():
        xn_ref[...] = _rmsnorm_val(x_ref[...], g_ref[...]).astype(BF16)

    def project():
        y = _nt_dot(wt_ref[...], xn_ref[...])
        return y.reshape(y.shape[0] // HEAD_DIM, HEAD_DIM, y.shape[1])

    @pl.when(j == 0)
    def _():
        y = project()
        cos, sin = cos_ref[...], sin_ref[...]
        x1, x2 = y[:, :half, :], y[:, half:, :]
        k_ref[0] = jnp.concatenate([x1 * cos - x2 * sin, x2 * cos + x1 * sin], axis=1)

    @pl.when(j == 1)
    def _():
        v_ref[0] = project()


def _proj_feature_major(x, gain, wt_bf16, w_block, tables_fm, n_seq, seq_len, keep, *, tr):
    d = x.shape[1]
    width = 1024
    n_heads = width // HEAD_DIM
    tiles = keep // tr
    seq_tiles = seq_len // tr
    first = seq_tiles - tiles

    def rows(i):
        return (i // tiles) * seq_tiles + first + (i % tiles)

    out_spec = pl.BlockSpec((1, n_heads, HEAD_DIM, tr), lambda i, j: (i // tiles, 0, 0, i % tiles))
    out_sds = jax.ShapeDtypeStruct((n_seq, n_heads, HEAD_DIM, keep), F32)
    tab_spec = pl.BlockSpec((HEAD_DIM // 2, tr), lambda i, j: (0, rows(i)))
    return pl.pallas_call(
        _proj_fm_kernel,
        grid=(n_seq * tiles, 2),
        in_specs=[
            pl.BlockSpec((tr, d), lambda i, j: (rows(i), 0)),
            pl.BlockSpec((1, d), lambda i, j: (0, 0)),
            pl.BlockSpec((width, d), lambda i, j: (w_block + j, 0)),
            tab_spec, tab_spec,
        ],
        out_specs=[out_spec, out_spec],
        out_shape=[out_sds, out_sds],
        scratch_shapes=[pltpu.VMEM((tr, d), BF16)],
        compiler_params=_cparams(("parallel", "arbitrary")),
        name="proj_feature_major",
    )(x, gain.reshape(1, d), wt_bf16, *tables_fm)


DIL_CHUNK_ROWS = 2048
DIL_HEAD_LANES = LANES


def _dil_attn_kernel(q_ref, k_ref, v_ref, kp_ref, vp_ref, o_ref, l_ref, *, dil, n_back):
    i = pl.program_id(1)
    blk = BAND_BLOCK
    span = blk * dil
    n_units = q_ref.shape[0] // blk
    pair_w = HEADS_PER_VREG * blk
    ki = lax.broadcasted_iota(jnp.int32, (2 * blk, pair_w), 0)
    qi = (lax.broadcasted_iota(jnp.int32, (2 * blk, pair_w), 1) & (blk - 1)) + blk
    dist = qi - ki
    in_band = (dist >= 0) & (dist <= n_back)
    is_cur = ki >= blk
    first_head = lax.broadcasted_iota(jnp.int32, (blk, LANES), 1) < HEAD_DIM

    def rows(ref, start):
        if dil == 1:
            return ref[pl.ds(pl.multiple_of(start, blk), blk), :]
        return ref[pl.ds(start, blk, stride=dil), :]

    def unit_body(u, carry):
        sb = u // dil
        r = u % dil
        start = sb * span + r
        prev_here = sb > 0
        prev_start = jnp.maximum(start - span, 0)
        q2 = rows(q_ref, start) * ATTN_SCALE
        q_bd = jnp.concatenate([jnp.where(first_head, q2, 0.0), jnp.where(first_head, 0.0, q2)], axis=0).astype(BF16)
        k_prev = jnp.where(prev_here, rows(k_ref, prev_start), rows(kp_ref, r))
        v_prev = jnp.where(prev_here, rows(v_ref, prev_start), rows(vp_ref, r))
        k2 = jnp.concatenate([k_prev, rows(k_ref, start)], axis=0).astype(BF16)
        v2 = jnp.concatenate([v_prev, rows(v_ref, start)], axis=0).astype(BF16)
        mask = in_band & (is_cur | (prev_here | (i > 0)))
        s = jnp.where(mask, _nt_dot(k2, q_bd), -jnp.inf)
        m = jnp.max(s, axis=0, keepdims=True)
        e = jnp.exp(s - m)
        den = jnp.sum(e, axis=0, keepdims=True)
        prob = (e * (1.0 / den)).astype(BF16)
        lse = m + jnp.log(den)
        outs, lses = [], []
        for h in range(HEADS_PER_VREG):
            outs.append(lax.dot_general(prob[:, h * blk:(h + 1) * blk], v2[:, h * HEAD_DIM:(h + 1) * HEAD_DIM],
                                        (((0,), (0,)), ((), ())), preferred_element_type=F32))
            lses.append(jnp.broadcast_to(lse[:, h * blk:(h + 1) * blk], (HEAD_DIM, blk)))
        o_val = jnp.concatenate(outs, axis=1)
        l_val = jnp.concatenate(lses, axis=0).T
        if dil == 1:
            o_ref[pl.ds(pl.multiple_of(start, blk), blk), :] = o_val
            l_ref[pl.ds(pl.multiple_of(start, blk), blk), :] = l_val
        else:
            o_ref[pl.ds(start, blk, stride=dil), :] = o_val
            l_ref[pl.ds(start, blk, stride=dil), :] = l_val
        return carry

    lax.fori_loop(0, n_units, unit_body, 0, unroll=True)


def _dilated_attention_prompt(qkv, group, n_seq, seq_len, width, n_groups):
    window, dil = DIL_GROUPS[group]
    rows, hl = DIL_CHUNK_ROWS, DIL_HEAD_LANES
    span = BAND_BLOCK * dil
    chunks = seq_len // rows
    per_w = width // hl
    base = group * 3

    def cur(which):
        return pl.BlockSpec((rows, hl), lambda b, i, c: (b * chunks + i, (base + which) * per_w + c))

    def prev(which):
        return pl.BlockSpec(
            (span, hl),
            lambda b, i, c: (jnp.maximum((b * chunks + i) * (rows // span) - 1, 0), (base + which) * per_w + c))

    out_spec = pl.BlockSpec((rows, hl), lambda b, i, c: (b * chunks + i, c))
    out_sds = jax.ShapeDtypeStruct((n_seq * seq_len, width), F32)
    kern = functools.partial(_dil_attn_kernel, dil=dil, n_back=window // dil)
    return pl.pallas_call(
        kern,
        grid=(n_seq, chunks, per_w),
        in_specs=[cur(0), cur(1), cur(2), prev(1), prev(2)],
        out_specs=[out_spec, out_spec],
        out_shape=[out_sds, out_sds],
        compiler_params=_cparams(("parallel", "parallel", "parallel")),
        name="dilated_attn_prompt",
    )(qkv, qkv, qkv, qkv, qkv)


def _merge_oproj_kernel(*refs, n_groups):
    x_ref, w_ref = refs[0], refs[1]
    o_refs = refs[2:2 + n_groups]
    l_refs = refs[2 + n_groups:2 + 2 * n_groups]
    out_ref = refs[-1]
    if n_groups == 1:
        o = o_refs[0][...]
    else:
        ls = [r[...] for r in l_refs]
        m = functools.reduce(jnp.maximum, ls)
        es = [jnp.exp(l - m) for l in ls]
        den = functools.reduce(lambda a, b: a + b, es)
        o = None
        for e, o_ref in zip(es, o_refs):
            term = (e / den) * o_ref[...]
            o = term if o is None else o + term
    out_ref[...] = x_ref[...] + jnp.dot(o.astype(BF16), w_ref[...], preferred_element_type=F32)


def _merge_oproj(x, w_bf16, outs, lses, *, tm):
    t, d = x.shape
    width = w_bf16.shape[0]
    n_groups = len(outs)
    row_spec = pl.BlockSpec((tm, width), lambda i: (i, 0))
    kern = functools.partial(_merge_oproj_kernel, n_groups=n_groups)
    return pl.pallas_call(
        kern,
        grid=(t // tm,),
        in_specs=[pl.BlockSpec((tm, d), lambda i: (i, 0)), pl.BlockSpec((width, d), lambda i: (0, 0))]
        + [row_spec] * (n_groups + len(lses)),
        out_specs=pl.BlockSpec((tm, d), lambda i: (i, 0)),
        out_shape=jax.ShapeDtypeStruct((t, d), F32),
        compiler_params=_cparams(("parallel",)),
        name="merge_oproj",
    )(x, w_bf16, *outs, *lses)


FFN_CHUNK = 256


def _ffn_kernel(*refs, conv_shift, tiles_per_seq, final):
    it = iter(refs)
    h_ref, prev_ref, gn_ref, wg_ref, wu_ref, wd_ref, cw_ref, cb_ref = (next(it) for _ in range(8))
    gf_ref = next(it) if final else None
    out_ref, tail_ref = next(it), next(it)
    y_ref = next(it) if final else None
    xn_ref, act_ref = next(it), next(it)

    i = pl.program_id(0)
    tm = h_ref.shape[0]
    n_chunks = wg_ref.shape[1] // FFN_CHUNK
    x = h_ref[...]
    xn_ref[...] = _rmsnorm_val(x, gn_ref[...]).astype(BF16)
    if conv_shift == 1:
        xn_prev = _rmsnorm_val(prev_ref[...], gn_ref[...]).astype(BF16)
        has_prev = (i % tiles_per_seq) != 0
        row = lax.broadcasted_iota(jnp.int32, (tm, FFN_CHUNK), 0)

    def chunk_body(c, carry):
        off = pl.multiple_of(c * FFN_CHUNK, FFN_CHUNK)
        wg = wg_ref[:, pl.ds(off, FFN_CHUNK)]
        g = jnp.dot(xn_ref[...], wg, preferred_element_type=F32)
        u = jnp.dot(xn_ref[...], wu_ref[:, pl.ds(off, FFN_CHUNK)], preferred_element_type=F32)
        if conv_shift == 1:
            gp = jnp.dot(xn_prev, wg, preferred_element_type=F32)
            gp = jnp.where(has_prev, gp, 0.0)
            last, last2 = gp[SUBLANES - 1:SUBLANES, :], gp[SUBLANES - 2:SUBLANES - 1, :]
            g1 = jnp.where(row == 0, last, pltpu.roll(g, 1, 0))
            g2 = jnp.where(row == 0, last2, jnp.where(row == 1, last, pltpu.roll(g, 2, 0)))
            tail_ref[0, :, pl.ds(off, FFN_CHUNK)] = g[tm - SUBLANES:, :]
        else:
            ext = jnp.concatenate([prev_ref[:, pl.ds(off, FFN_CHUNK)], g], axis=0)
            g2 = ext[:tm, :]
            g1 = ext[conv_shift:conv_shift + tm, :]
            tail_ref[:, pl.ds(off, FFN_CHUNK)] = ext[tm:, :]
        cw = cw_ref[:, pl.ds(off, FFN_CHUNK)]
        conv = cb_ref[:, pl.ds(off, FFN_CHUNK)] + ((cw[0:1, :] * g2 + cw[1:2, :] * g1) + cw[2:3, :] * g)
        act = (conv * (1.0 / (1.0 + jnp.exp(-conv)))) * u
        act_ref[:, pl.ds(off, FFN_CHUNK)] = act.astype(BF16)
        return carry

    lax.fori_loop(0, n_chunks, chunk_body, 0, unroll=True)
    h_out = x + jnp.dot(act_ref[...], wd_ref[...], preferred_element_type=F32)
    out_ref[...] = h_out
    if final:
        y_ref[...] = _rmsnorm_val(h_out, gf_ref[...])


def _conv_ffn(h, prev, gain, wg, wu, wd, conv_w, conv_b, final_gain, *, tm, conv_shift, seq_len):
    t, d = h.shape
    d_ff = wg.shape[1]
    final = final_gain is not None
    const2 = lambda i: (0, 0)
    in_specs = [pl.BlockSpec((tm, d), lambda i: (i, 0))]
    args = [h]
    if conv_shift == 1:
        rows8 = tm // SUBLANES
        in_specs.append(pl.BlockSpec((SUBLANES, d), lambda i: (jnp.maximum(i * rows8 - 1, 0), 0)))
        args.append(h)
        tail_shape = jax.ShapeDtypeStruct((t // tm, SUBLANES, d_ff), F32)
        tail_spec = pl.BlockSpec((1, SUBLANES, d_ff), lambda i: (i, 0, 0))
        tiles_per_seq = seq_len // tm
    else:
        assert t == tm
        n_prev = (CONV_WIDTH - 1) * conv_shift
        in_specs.append(pl.BlockSpec((n_prev, d_ff), const2))
        args.append(prev)
        tail_shape = jax.ShapeDtypeStruct((n_prev, d_ff), F32)
        tail_spec = pl.BlockSpec((n_prev, d_ff), const2)
        tiles_per_seq = 1
    resident = pl.Buffered(1)
    in_specs += [
        pl.BlockSpec((1, d), const2),
        pl.BlockSpec((d, d_ff), const2, pipeline_mode=resident),
        pl.BlockSpec((d, d_ff), const2, pipeline_mode=resident),
        pl.BlockSpec((d_ff, d), const2, pipeline_mode=resident),
        pl.BlockSpec((CONV_WIDTH, d_ff), const2), pl.BlockSpec((1, d_ff), const2),
    ]
    args += [gain.reshape(1, d), wg, wu, wd, conv_w, conv_b.reshape(1, d_ff)]
    out_specs = [pl.BlockSpec((tm, d), lambda i: (i, 0)), tail_spec]
    out_shape = [jax.ShapeDtypeStruct((t, d), F32), tail_shape]
    if final:
        in_specs.append(pl.BlockSpec((1, d), const2))
        args.append(final_gain.reshape(1, d))
        out_specs.append(pl.BlockSpec((tm, d), lambda i: (i, 0)))
        out_shape.append(jax.ShapeDtypeStruct((t, d), F32))
    kern = functools.partial(_ffn_kernel, conv_shift=conv_shift, tiles_per_seq=tiles_per_seq, final=final)
    return pl.pallas_call(
        kern,
        grid=(t // tm,),
        in_specs=in_specs,
        out_specs=out_specs,
        out_shape=out_shape,
        scratch_shapes=[pltpu.VMEM((tm, d), BF16), pltpu.VMEM((tm, d_ff), BF16)],
        compiler_params=_cparams(("parallel",)),
        name="conv_ffn",
    )(*args)


def _block_mean_kernel(k_ref, o_ref):
    for r in range(o_ref.shape[0]):
        o_ref[r:r + 1, :] = jnp.mean(k_ref[r * MOBA_BLOCK:(r + 1) * MOBA_BLOCK, :], axis=0, keepdims=True)


def _block_mean(k):
    t, width = k.shape
    rows = SUBLANES * MOBA_BLOCK
    return pl.pallas_call(
        _block_mean_kernel,
        grid=(t // rows,),
        in_specs=[pl.BlockSpec((rows, width), lambda i: (i, 0))],
        out_specs=pl.BlockSpec((SUBLANES, width), lambda i: (i, 0)),
        out_shape=jax.ShapeDtypeStruct((t // MOBA_BLOCK, width), F32),
        compiler_params=_cparams(("parallel",)),
        name="moba_block_mean",
    )(k)


def _top_k_mask(gate, n_valid, k_sel, axis):
    nb = gate.shape[axis]
    blk_id = lax.broadcasted_iota(jnp.int32, gate.shape, axis)
    sel = jnp.zeros(gate.shape, F32)
    g = gate
    for kk in range(k_sel):
        mx = jnp.max(g, axis=axis, keepdims=True)
        idx = jnp.min(jnp.where(g == mx, blk_id, nb), axis=axis, keepdims=True)
        hit = blk_id == idx
        counts = jnp.where(kk < n_valid, 1.0, 0.0)
        sel = jnp.maximum(sel, jnp.where(hit, counts, 0.0))
        g = jnp.where(hit, -jnp.inf, g)
    return sel


def _moba_prompt_kernel(qi_tab, kj_tab, q_ref, k_ref, vt_ref, km_ref, *rest,
                        n_pairs, k_sel, steps_per_seq, riders):
    n_in, n_out = 7 * len(riders), 4 * len(riders)
    rider_in = rest[:n_in]
    o_ref = rest[n_in]
    rider_out = rest[n_in + 1:n_in + 1 + n_out]
    qbd_ref, sel_ref, m_ref, l_ref, a_ref, acc_ref, s_ref, p_ref = rest[n_in + 1 + n_out:]
    step = pl.program_id(0)

    for r, (rider_steps, rider_chunks, rider_cfg) in enumerate(riders):
        def run_rider(r=r, rider_chunks=rider_chunks, rider_cfg=rider_cfg):
            _sample_window_body(step // rider_chunks, *rider_in[7 * r:7 * r + 7], *rider_out[4 * r:4 * r + 4],
                                **dict(rider_cfg))

        pl.when(step < rider_steps)(run_rider)

    t = step % steps_per_seq
    qi = qi_tab[t]
    kj = kj_tab[t]
    blk = MOBA_BLOCK
    nb = km_ref.shape[1]
    is_own = kj == qi

    pair_w = HEADS_PER_VREG * blk

    @pl.when(kj == 0)
    def _():
        m_ref[...] = jnp.full(m_ref.shape, NEG_BIG, F32)
        l_ref[...] = jnp.zeros_like(l_ref)
        acc_ref[...] = jnp.zeros_like(acc_ref)
        blk_id = lax.broadcasted_iota(jnp.int32, (nb, pair_w), 0)
        feat = lax.broadcasted_iota(jnp.int32, (LANES, blk), 0)
        for p in range(n_pairs):
            q_t = q_ref[:, p * LANES:(p + 1) * LANES].T
            q_bd = jnp.concatenate([jnp.where(feat < HEAD_DIM, q_t, 0.0),
                                    jnp.where(feat < HEAD_DIM, 0.0, q_t)], axis=1)
            qbd_ref[p] = (q_bd * ATTN_SCALE).astype(BF16)
            gate = _dot_precise(km_ref[0, :, p * LANES:(p + 1) * LANES], q_bd)
            gate = jnp.where(blk_id < qi, gate, -jnp.inf)
            sel_ref[p] = _top_k_mask(gate, qi, k_sel, 0)

    for p in range(n_pairs):
        s_ref[p] = jnp.dot(k_ref[:, p * LANES:(p + 1) * LANES].astype(BF16), qbd_ref[p],
                           preferred_element_type=F32)

    @pl.when(is_own)
    def _():
        key = lax.broadcasted_iota(jnp.int32, (blk, pair_w), 0)
        qry = lax.broadcasted_iota(jnp.int32, (blk, pair_w), 1) & (blk - 1)
        bias = jnp.where(key > qry, MASK_BIAS, 0.0)
        for p in range(n_pairs):
            s_ref[p] = s_ref[p] + bias

    takes_part = jnp.where(is_own, 1.0, sel_ref[:, pl.ds(kj, 1), :]) > 0.5
    s = s_ref[...]
    m_old = m_ref[...]
    m_new = jnp.where(takes_part, jnp.maximum(m_old, jnp.max(s, axis=1, keepdims=True)), m_old)
    alpha = jnp.exp(m_old - m_new)
    prob = jnp.exp(s - jnp.where(takes_part, m_new, -NEG_BIG))
    l_ref[...] = alpha * l_ref[...] + jnp.sum(prob, axis=1, keepdims=True)
    m_ref[...] = m_new
    a_ref[...] = alpha
    p_ref[...] = prob.astype(BF16)
    for h in range(n_pairs * HEADS_PER_VREG):
        p, c = divmod(h, HEADS_PER_VREG)
        qs = slice(c * blk, (c + 1) * blk)
        acc_ref[h] = a_ref[p, :, qs] * acc_ref[h] + jnp.dot(vt_ref[0, h].astype(BF16), p_ref[p, :, qs],
                                                             preferred_element_type=F32)

    @pl.when(is_own)
    def _():
        for p in range(n_pairs):
            o_t = jnp.concatenate(
                [acc_ref[p * HEADS_PER_VREG + c] / l_ref[p, :, c * blk:(c + 1) * blk] for c in range(HEADS_PER_VREG)],
                axis=0)
            o_ref[:, p * LANES:(p + 1) * LANES] = o_t.T


def _moba_prompt(q, k, v_t, kmean, n_seq, seq_len, width, riders=()):
    nb = seq_len // MOBA_BLOCK
    n_heads = width // HEAD_DIM
    pairs = [(a, b) for a in range(nb) for b in range(a + 1)]
    steps = len(pairs)
    qi_tab = jnp.asarray([p[0] for p in pairs], jnp.int32)
    kj_tab = jnp.asarray([p[1] for p in pairs], jnp.int32)
    n_pairs = width // LANES
    pair_w = HEADS_PER_VREG * MOBA_BLOCK
    plans = []
    for rider in riders:
        n_heads_r = rider[2].shape[3]
        chunks_r = n_heads_r // rider[-1]
        steps_r = rider[2].shape[1] * chunks_r
        assert steps_r <= n_seq * steps

        def wrap(f, chunks_r=chunks_r, steps_r=steps_r):
            def index_map(s, qt, kt):
                r = jnp.minimum(s, steps_r - 1)
                return f(r // chunks_r, r % chunks_r)
            return index_map

        plans.append(_sample_window_plan(*rider, wrap))
    kern = functools.partial(
        _moba_prompt_kernel, n_pairs=n_pairs, k_sel=min(MOBA_TOP_K, nb), steps_per_seq=steps,
        riders=tuple((p["n_seq"] * p["chunks"], p["chunks"], tuple(sorted(p["cfg"].items()))) for p in plans))
    rider_in_specs = [s for p in plans for s in p["in_specs"]]
    rider_out_specs = [s for p in plans for s in p["out_specs"]]
    rider_out_shape = [s for p in plans for s in p["out_shape"]]
    rider_args = [a for p in plans for a in p["args"]]
    o_spec = pl.BlockSpec((MOBA_BLOCK, width), lambda s, qt, kt: ((s // steps) * nb + qt[s % steps], 0))
    o_sds = jax.ShapeDtypeStruct((n_seq * seq_len, width), F32)
    grid_spec = pltpu.PrefetchScalarGridSpec(
        num_scalar_prefetch=2,
        grid=(n_seq * steps,),
        in_specs=[
            pl.BlockSpec((MOBA_BLOCK, width), lambda s, qt, kt: ((s // steps) * nb + qt[s % steps], 0)),
            pl.BlockSpec((MOBA_BLOCK, width), lambda s, qt, kt: ((s // steps) * nb + kt[s % steps], 0)),
            pl.BlockSpec((1, n_heads, HEAD_DIM, MOBA_BLOCK), lambda s, qt, kt: (s // steps, 0, 0, kt[s % steps])),
            pl.BlockSpec((1, nb, width), lambda s, qt, kt: (s // steps, 0, 0)),
        ] + rider_in_specs,
        out_specs=[o_spec] + rider_out_specs,
        scratch_shapes=[
            pltpu.VMEM((n_pairs, LANES, pair_w), BF16),
            pltpu.VMEM((n_pairs, nb, pair_w), F32),
            pltpu.VMEM((n_pairs, 1, pair_w), F32),
            pltpu.VMEM((n_pairs, 1, pair_w), F32),
            pltpu.VMEM((n_pairs, 1, pair_w), F32),
            pltpu.VMEM((n_heads, HEAD_DIM, MOBA_BLOCK), F32),
            pltpu.VMEM((n_pairs, MOBA_BLOCK, pair_w), F32),
            pltpu.VMEM((n_pairs, MOBA_BLOCK, pair_w), BF16),
        ],
    )
    return pl.pallas_call(
        kern,
        grid_spec=grid_spec,
        out_shape=[o_sds] + rider_out_shape,
        compiler_params=_cparams(("arbitrary",)),
        name="moba_prompt",
    )(qi_tab, kj_tab, q, k, v_t, kmean, *rider_args)


SAMPLE_STEP_BYTES = 2 * 1024 * 1024
RIDER_BLOCK_BYTES = 1024 * 1024
NEW_ROWS_PER_LANE_TILE = 32


def _sample_window_kernel(*refs, **cfg):
    _sample_window_body(pl.program_id(0), *refs, **cfg)


def _sample_window_body(n, q_ref, kn_ref, vn_ref, knt_ref, vnt_ref, kc_ref, vc_ref,
                        ko_ref, vo_ref, o_ref, l_ref, *, window, dil, n_new):
    hc, hd, n_buf = kc_ref.shape[2:]
    qp = q_ref.shape[3]
    shift = (LANES - n_new - (n % NEW_ROWS_PER_LANE_TILE) * n_new) % LANES
    lane = lax.broadcasted_iota(jnp.int32, (hc * hd, LANES), 1)
    is_new_lane = lane >= LANES - n_new

    for c_ref, nt_ref, out_ref in ((kc_ref, knt_ref, ko_ref), (vc_ref, vnt_ref, vo_ref)):
        rolled = pltpu.roll(c_ref[0, 0].reshape(hc * hd, n_buf), n_buf - n_new, 1)
        new_t = pltpu.roll(nt_ref[...].reshape(hc * hd, LANES), shift, 1)
        if n_buf > LANES:
            out_ref[0, 0, :, :, :n_buf - LANES] = rolled[:, :n_buf - LANES].reshape(hc, hd, n_buf - LANES)
        last = jnp.where(is_new_lane, new_t, rolled[:, n_buf - LANES:])
        out_ref[0, 0, :, :, n_buf - LANES:] = last.reshape(hc, hd, LANES)

    q_idx = lax.broadcasted_iota(jnp.int32, (qp, n_buf), 0)
    r_idx = lax.broadcasted_iota(jnp.int32, (qp, n_buf), 1)
    dist_c = n_buf + q_idx - r_idx
    dil_mask = dil - 1
    valid_c = ((dist_c & dil_mask) == 0) & (dist_c <= window)
    qn_idx = lax.broadcasted_iota(jnp.int32, (qp, qp), 0)
    jn_idx = lax.broadcasted_iota(jnp.int32, (qp, qp), 1)
    dist_n = qn_idx - jn_idx
    valid_n = (dist_n >= 0) & ((dist_n & dil_mask) == 0) & (dist_n <= window) & ((jn_idx < n_new) | (jn_idx == qn_idx))

    q = q_ref[0, 0].astype(BF16)
    s_c = _bdot(q, kc_ref[0, 0].astype(BF16), 2, 1) * ATTN_SCALE
    s_c = jnp.where(valid_c, s_c, -jnp.inf)
    s_n = _bdot(q, kn_ref[0, 0].astype(BF16), 2, 2) * ATTN_SCALE
    s_n = jnp.where(valid_n, s_n, -jnp.inf)
    m = jnp.maximum(jnp.max(s_c, axis=-1, keepdims=True), jnp.max(s_n, axis=-1, keepdims=True))
    e_c = jnp.exp(s_c - m)
    e_n = jnp.exp(s_n - m)
    den = jnp.sum(e_c, axis=-1, keepdims=True) + jnp.sum(e_n, axis=-1, keepdims=True)
    o = _bdot(e_c.astype(BF16), vc_ref[0, 0].astype(BF16), 2, 2)
    o = o + _bdot(e_n.astype(BF16), vn_ref[0, 0].astype(BF16), 2, 1)
    o_ref[0] = o / den
    l_ref[0] = jnp.broadcast_to(m + jnp.log(den), o.shape)


def _sample_window_attention(qkv_hm, qkv_t3, cache_k, cache_v, group, n_new):
    _, n_seq, n_buf, n_heads, hd = cache_k.shape
    hc = max(1, min(n_heads, SAMPLE_STEP_BYTES // (hd * n_buf * 4)))
    plan = _sample_window_plan(qkv_hm, qkv_t3, cache_k, cache_v, group, n_new, hc, lambda f: f)
    kern = functools.partial(_sample_window_kernel, **plan["cfg"])
    results = pl.pallas_call(
        kern,
        grid=(n_seq, plan["chunks"]),
        in_specs=plan["in_specs"],
        out_specs=plan["out_specs"],
        out_shape=plan["out_shape"],
        compiler_params=_cparams(("parallel", "parallel")),
        name="sample_window_attn",
    )(*plan["args"])
    return _sample_window_results(results)


def _sample_window_results(results):
    ko, vo, o, lse = results
    back = (0, 1, 4, 2, 3)
    return o, lse, jnp.transpose(ko, back), jnp.transpose(vo, back)


def _sample_window_plan(qkv_hm, qkv_t3, cache_k, cache_v, group, n_new, hc, wrap):
    window, dil = DIL_GROUPS[group]
    _, n_seq, n_buf, n_heads, hd = cache_k.shape
    qp = qkv_hm.shape[3]
    chunks = n_heads // hc
    ck = jnp.transpose(cache_k, (0, 1, 3, 4, 2))
    cv = jnp.transpose(cache_v, (0, 1, 3, 4, 2))
    base = group * 3

    def hm_spec(which):
        return pl.BlockSpec((1, 1, hc, qp, hd), wrap(lambda n, c: (base + which, n, c, 0, 0)))

    def t_spec(which):
        return pl.BlockSpec((hc, hd, LANES),
                            wrap(lambda n, c: ((base + which) * chunks + c, 0, n // NEW_ROWS_PER_LANE_TILE)))

    cache_spec = pl.BlockSpec((1, 1, hc, hd, n_buf), wrap(lambda n, c: (0, n, c, 0, 0)))
    out_spec = pl.BlockSpec((1, hc, qp, hd), wrap(lambda n, c: (n, c, 0, 0)))
    out_sds = jax.ShapeDtypeStruct((n_seq, n_heads, qp, hd), F32)
    return dict(
        args=[qkv_hm, qkv_hm, qkv_hm, qkv_t3, qkv_t3, ck, cv],
        in_specs=[hm_spec(0), hm_spec(1), hm_spec(2), t_spec(1), t_spec(2), cache_spec, cache_spec],
        out_specs=[cache_spec, cache_spec, out_spec, out_spec],
        out_shape=[jax.ShapeDtypeStruct(ck.shape, F32), jax.ShapeDtypeStruct(cv.shape, F32), out_sds, out_sds],
        cfg=dict(window=window, dil=dil, n_new=n_new), chunks=chunks, n_seq=n_seq)


def _moba_sample_kernel(pt_ref, q_ref, kn_ref, vn_ref, *refs, n_new, k_sel, n_pages, pages_per_block):
    k_refs, v_refs, o_ref = refs[:n_pages], refs[n_pages:2 * n_pages], refs[2 * n_pages]
    n_heads, qp, _ = q_ref.shape[1:]
    n_blocks = n_pages // pages_per_block
    q = q_ref[0].astype(BF16)
    raw = [_bdot(q, k_ref[0].astype(BF16), 2, 1) for k_ref in k_refs]
    blk_lane = lax.broadcasted_iota(jnp.int32, (n_heads, qp, LANES), 2)
    gate = jnp.full((n_heads, qp, LANES), -jnp.inf, F32)
    for bb in range(n_blocks):
        pages = raw[bb * pages_per_block:(bb + 1) * pages_per_block]
        total = functools.reduce(lambda a, b: a + b, [jnp.sum(r, axis=-1, keepdims=True) for r in pages])
        gate = jnp.where(blk_lane == bb, total / MOBA_BLOCK, gate)
    picked = _top_k_mask(gate, n_blocks, k_sel, 2)
    takes_part = [jnp.max(jnp.where(blk_lane == bb, picked, 0.0), axis=-1, keepdims=True) > 0.5
                  for bb in range(n_blocks)]

    qn_idx = lax.broadcasted_iota(jnp.int32, (qp, qp), 0)
    jn_idx = lax.broadcasted_iota(jnp.int32, (qp, qp), 1)
    valid_own = (jn_idx <= qn_idx) & ((jn_idx < n_new) | (jn_idx == qn_idx))
    s_own = jnp.where(valid_own, _bdot(q, kn_ref[0].astype(BF16), 2, 2) * ATTN_SCALE, -jnp.inf)
    m_all = jnp.max(s_own, axis=-1, keepdims=True)
    for j, r in enumerate(raw):
        page_max = jnp.max(r, axis=-1, keepdims=True) * ATTN_SCALE
        m_all = jnp.maximum(m_all, jnp.where(takes_part[j // pages_per_block], page_max, NEG_BIG))
    e_own = jnp.exp(s_own - m_all)
    den = jnp.sum(e_own, axis=-1, keepdims=True)
    num = _bdot(e_own.astype(BF16), vn_ref[0].astype(BF16), 2, 1)
    for j, (r, v_ref) in enumerate(zip(raw, v_refs)):
        e = jnp.where(takes_part[j // pages_per_block], jnp.exp(r * ATTN_SCALE - m_all), 0.0)
        den = den + jnp.sum(e, axis=-1, keepdims=True)
        num = num + _bdot(e.astype(BF16), v_ref[0].astype(BF16), 2, 2)
    o_ref[0] = num / den


def _moba_sample(q_hm, k_hm, v_hm, cache_k, cache_v, page_table, n_new):
    n_seq, n_pages = page_table.shape
    _, n_heads, qp, hd = q_hm.shape
    pages_per_block = MOBA_BLOCK // PAGE_SIZE
    assert pages_per_block == 2 and n_pages % pages_per_block == 0 and n_new <= SAMPLE_Q_PAD
    n_blocks = n_pages // pages_per_block
    ck = jnp.transpose(cache_k, (0, 2, 3, 1))
    cv = jnp.transpose(cache_v, (0, 2, 3, 1))
    pt = page_table.reshape(-1).astype(jnp.int32)

    def page_spec(j):
        return pl.BlockSpec((1, n_heads, hd, PAGE_SIZE), lambda n, pt_ref: (pt_ref[n * n_pages + j], 0, 0, 0))

    seq_spec = pl.BlockSpec((1, n_heads, qp, hd), lambda n, pt_ref: (n, 0, 0, 0))
    page_specs = [page_spec(j) for j in range(n_pages)]
    kern = functools.partial(_moba_sample_kernel, n_new=n_new, k_sel=min(MOBA_TOP_K, n_blocks + 1),
                             n_pages=n_pages, pages_per_block=pages_per_block)
    grid_spec = pltpu.PrefetchScalarGridSpec(
        num_scalar_prefetch=1,
        grid=(n_seq,),
        in_specs=[seq_spec, seq_spec, seq_spec] + page_specs + page_specs,
        out_specs=seq_spec,
    )
    return pl.pallas_call(
        kern,
        grid_spec=grid_spec,
        out_shape=jax.ShapeDtypeStruct((n_seq, n_heads, qp, hd), F32),
        compiler_params=_cparams(("parallel",)),
        name="moba_sample",
    )(pt, q_hm, k_hm, v_hm, *([ck] * n_pages), *([cv] * n_pages))


def _row_tile(t, cap):
    tm = min(t, cap)
    assert t % tm == 0
    return tm


def kernel(x_prompt, x_sample, cache_a_k0, cache_a_v0, cache_a_k1, cache_a_v1, cache_a_k2, cache_a_v2, cache_b_k, cache_b_v, state_ffn_conv, page_table, norm_mix, norm_ffn, norm_kv, norm_final, w_qkv_a, w_o_a, w_kv_b, w_q_b, w_o_b, w_ffn_gate, w_ffn_up, ffn_conv_w, ffn_conv_b, w_ffn_down):
    n_p, seq, d = x_prompt.shape
    n_s, dec_seq, _ = x_sample.shape
    n_groups = len(DIL_GROUPS)
    width = w_o_a.shape[1]
    n_heads = width // HEAD_DIM
    d_ff = w_ffn_gate.shape[2]
    past_len = page_table.shape[1] * PAGE_SIZE
    assert w_qkv_a.shape[0] == 1 and w_q_b.shape[0] == 1, "one self-decoder and one cross-decoder layer"
    assert width == 1024 and seq % DIL_CHUNK_ROWS == 0 and seq % (SUBLANES * MOBA_BLOCK) == 0
    assert (n_s * dec_seq) % LANES == 0 and n_s % NEW_ROWS_PER_LANE_TILE == 0
    assert NEW_ROWS_PER_LANE_TILE * dec_seq == LANES and d_ff % FFN_CHUNK == 0
    assert all(dil & (dil - 1) == 0 and window // dil == BAND_BLOCK and BAND_BLOCK * dil <= DIL_CHUNK_ROWS
               for window, dil in DIL_GROUPS)

    wqkv = w_qkv_a[0].astype(BF16)
    wqkv_t = w_qkv_a[0].T.astype(BF16)
    wo_a = w_o_a[0].astype(BF16)
    wkv = w_kv_b.astype(BF16)
    wkv_t = w_kv_b.T.astype(BF16)
    wq_b = w_q_b[0].astype(BF16)
    wo_b = w_o_b[0].astype(BF16)
    wg = w_ffn_gate.astype(BF16)
    wu = w_ffn_up.astype(BF16)
    wd = w_ffn_down.astype(BF16)

    t_s = n_s * dec_seq
    xs = jnp.transpose(x_sample, (1, 0, 2)).reshape(t_s, d)
    pos_s = past_len + jnp.repeat(jnp.arange(dec_seq, dtype=jnp.int32), n_s)
    tab_s, _ = _rope_tables(pos_s)

    def to_head_major(a):
        c = a.shape[-1] // width
        a = jnp.transpose(a.reshape(dec_seq, n_s, c, n_heads, HEAD_DIM), (2, 1, 3, 0, 4))
        return jnp.pad(a, ((0, 0), (0, 0), (0, 0), (0, SAMPLE_Q_PAD - dec_seq), (0, 0)))

    def to_time_major(a):
        return jnp.transpose(a[:, :, :dec_seq], (2, 0, 1, 3)).reshape(t_s, width)

    def conv_prev(state):
        return jnp.transpose(state, (1, 0, 2)).reshape((CONV_WIDTH - 1) * n_s, d_ff)

    def conv_next(tail):
        return jnp.transpose(tail.reshape(CONV_WIDTH - 1, n_s, d_ff), (1, 0, 2))

    qkv_s = _norm_matmul(xs, norm_mix[0], wqkv, tab_s, rope_period=3, rope_count=2, tm=t_s)
    qkv_hm = to_head_major(qkv_s)
    qkv_t3 = jnp.transpose(qkv_s.reshape(dec_seq, n_s, n_groups * 3 * n_heads, HEAD_DIM), (2, 3, 1, 0))
    qkv_t3 = qkv_t3.reshape(n_groups * 3 * n_heads, HEAD_DIM, t_s)
    caches = ((cache_a_k0, cache_a_v0), (cache_a_k1, cache_a_v1), (cache_a_k2, cache_a_v2))

    t_p = n_p * seq
    xp = x_prompt.reshape(t_p, d)
    tab_p, tab_p_fm = _rope_tables(jnp.tile(jnp.arange(seq, dtype=jnp.int32), n_p))
    tm_mm = _row_tile(t_p, 1024)
    tm_p = _row_tile(seq, 512)

    qkv = _norm_matmul(xp, norm_mix[0], wqkv, tab_p, rope_period=3, rope_count=2, tm=tm_mm)
    outs, lses, pa = [], [], []
    for g, (window, _) in enumerate(DIL_GROUPS):
        o, lse = _dilated_attention_prompt(qkv, g, n_p, seq, width, n_groups)
        outs.append(o)
        lses.append(lse)
        keep = min(window, seq)
        k_t, v_t = _proj_feature_major(xp, norm_mix[0], wqkv_t, g * 3 + 1, tab_p_fm, n_p, seq, keep,
                                       tr=min(keep, 512))
        pa.append(jnp.transpose(k_t[None], (0, 1, 4, 2, 3)))
        pa.append(jnp.transpose(v_t[None], (0, 1, 4, 2, 3)))
    h = _merge_oproj(xp, wo_a, outs, lses, tm=tm_p)
    h, tail0 = _conv_ffn(h, None, norm_ffn[0], wg[0], wu[0], wd[0], ffn_conv_w[0], ffn_conv_b[0], None,
                         tm=tm_p, conv_shift=1, seq_len=seq)
    k_tok = _norm_matmul(h, norm_kv, wkv, tab_p, rope_period=1, rope_count=1, tm=tm_mm, n_out=width)
    kb_t, vb_t = _proj_feature_major(h, norm_kv, wkv_t, 0, tab_p_fm, n_p, seq, seq, tr=512)
    kmean = _block_mean(k_tok).reshape(n_p, seq // MOBA_BLOCK, width)
    qb = _norm_matmul(h, norm_mix[1], wq_b, tab_p, rope_period=1, rope_count=1, tm=tm_mm)
    n_moba_blocks = seq // MOBA_BLOCK
    moba_steps = n_p * n_moba_blocks * (n_moba_blocks + 1) // 2
    def rider_heads(g):
        return max(1, min(n_heads, RIDER_BLOCK_BYTES // (HEAD_DIM * caches[g][0].shape[2] * 4)))

    rider_groups = [g for g in range(n_groups) if n_s * (n_heads // rider_heads(g)) <= moba_steps]
    riders = [(qkv_hm, qkv_t3, caches[g][0], caches[g][1], g, dec_seq, rider_heads(g)) for g in rider_groups]
    o, *rider_results = _moba_prompt(qb, k_tok, vb_t, kmean, n_p, seq, width, riders=riders)
    h = _merge_oproj(h, wo_b, [o], [], tm=tm_p)
    h, tail1, y_p = _conv_ffn(h, None, norm_ffn[1], wg[1], wu[1], wd[1], ffn_conv_w[1], ffn_conv_b[1], norm_final,
                              tm=tm_p, conv_shift=1, seq_len=seq)
    y_p = y_p.reshape(n_p, seq, d)
    pb_k = jnp.transpose(kb_t, (0, 3, 1, 2))
    pb_v = jnp.transpose(vb_t, (0, 3, 1, 2))
    tiles_per_seq = seq // tm_p

    def prompt_conv_state(tail):
        last = tail.reshape(n_p, tiles_per_seq, SUBLANES, d_ff)[:, -1]
        return last[:, SUBLANES - (CONV_WIDTH - 1):]

    p_conv = jnp.stack([prompt_conv_state(tail0), prompt_conv_state(tail1)], axis=0)

    outs, lses, sa = [], [], []
    for g in range(n_groups):
        if g in rider_groups:
            r = rider_groups.index(g)
            o, lse, ko, vo = _sample_window_results(rider_results[4 * r:4 * r + 4])
        else:
            o, lse, ko, vo = _sample_window_attention(qkv_hm, qkv_t3, caches[g][0], caches[g][1], g, dec_seq)
        outs.append(to_time_major(o))
        lses.append(to_time_major(lse))
        sa += [ko, vo]
    hs = _merge_oproj(xs, wo_a, outs, lses, tm=t_s)
    hs, s_tail0 = _conv_ffn(hs, conv_prev(state_ffn_conv[0]), norm_ffn[0], wg[0], wu[0], wd[0], ffn_conv_w[0],
                            ffn_conv_b[0], None, tm=t_s, conv_shift=n_s, seq_len=dec_seq)
    kv_s = _norm_matmul(hs, norm_kv, wkv, tab_s, rope_period=2, rope_count=1, tm=t_s)
    qb_s = _norm_matmul(hs, norm_mix[1], wq_b, tab_s, rope_period=1, rope_count=1, tm=t_s)
    kv_hm = to_head_major(kv_s)
    o = _moba_sample(to_head_major(qb_s)[0], kv_hm[0], kv_hm[1], cache_b_k, cache_b_v, page_table, dec_seq)
    hs = _merge_oproj(hs, wo_b, [to_time_major(o)], [], tm=t_s)
    hs, s_tail1, y_s = _conv_ffn(hs, conv_prev(state_ffn_conv[1]), norm_ffn[1], wg[1], wu[1], wd[1], ffn_conv_w[1],
                                 ffn_conv_b[1], norm_final, tm=t_s, conv_shift=n_s, seq_len=dec_seq)
    y_s = jnp.transpose(y_s.reshape(dec_seq, n_s, d), (1, 0, 2))
    kv_s5 = jnp.transpose(kv_s.reshape(dec_seq, n_s, 2, n_heads, HEAD_DIM), (2, 1, 0, 3, 4))
    sb_k, sb_v = kv_s5[0], kv_s5[1]
    s_conv = jnp.stack([conv_next(s_tail0), conv_next(s_tail1)], axis=0)

    return (y_p, y_s, *pa, *sa, pb_k, pb_v, sb_k, sb_v, p_conv, s_conv)
```
